```python
import math
import jax, jax.numpy as jnp
from jax import lax
import numpy as np

D_MODEL = 1024
BATCH = 2
SEQ = 16384
DEPTH = 4

N_MIXERS = 4
D_FF = 2816
N_MOD = 9
EPS = 1e-6
Q_BLOCK = 128
GRID_W = 64

POOL_WINDOWS = (2, 4, 8, 16)
N_POOL = len(POOL_WINDOWS)
POOL_GROUP = D_MODEL // N_POOL

DIFF_HEADS = 8
DIFF_HEAD_DIM = D_MODEL // DIFF_HEADS // 2
DIFF_V_DIM = 2 * DIFF_HEAD_DIM
ROPE_THETA = 500000.0
ROT_DIM = DIFF_HEAD_DIM // 4

GQA_HEADS = 8
GQA_KV_HEADS = 2
GQA_HEAD_DIM = D_MODEL // GQA_HEADS
GQA_GROUP = GQA_HEADS // GQA_KV_HEADS
GQA_Q_DIM = GQA_HEADS * GQA_HEAD_DIM
GQA_KV_DIM = GQA_KV_HEADS * GQA_HEAD_DIM
AXIAL_THETA = 10000.0
AXIAL_DIM = GQA_HEAD_DIM // 2

CONV_WIDTH = 3

kernel_name = "hybrid_interleaved_adaln_encoder"


def rms_norm(x, g):
    xf = x.astype(jnp.float32)
    y = xf * lax.rsqrt(jnp.mean(xf * xf, axis=-1, keepdims=True) + EPS)
    return (y * g.astype(jnp.float32)).astype(x.dtype)


def modulate(h, shift, scale):
    return h * (1 + scale) + shift


def swiglu(h, w_gu, w_down):
    g, u = jnp.split(h @ w_gu, 2, axis=-1)
    return (jax.nn.silu(g) * u) @ w_down


def rope_tables(pos, dim, theta):
    inv = 1.0 / (theta ** (jnp.arange(0, dim, 2, dtype=jnp.float32) / dim))
    ang = pos.astype(jnp.float32)[:, None] * inv[None, :]
    return jnp.cos(ang), jnp.sin(ang)


def rope(x, cos, sin):
    half = x.shape[-1] // 2
    xf = x.astype(jnp.float32)
    x1, x2 = xf[..., :half], xf[..., half:]
    c = cos[None, :, None, :]
    s = sin[None, :, None, :]
    return jnp.concatenate([x1 * c - x2 * s, x2 * c + x1 * s], axis=-1).astype(x.dtype)


def sweep_query_blocks(fn, q):
    b, s = q.shape[:2]
    nblk = s // Q_BLOCK
    qb = jnp.moveaxis(q.reshape((b, nblk, Q_BLOCK) + q.shape[2:]), 1, 0)
    out = jnp.moveaxis(lax.map(fn, qb), 0, 1)
    return out.reshape((b, s) + out.shape[3:])


def pool_mixer(h, pool_w, pool_scale):
    b, s, d = h.shape
    cs = jnp.concatenate([jnp.zeros((b, 1, d), jnp.float32),
                          jnp.cumsum(h.astype(jnp.float32), axis=1)], axis=1)
    t = jnp.arange(s)
    diffs = []
    for g, win in enumerate(POOL_WINDOWS):
        lo = jnp.clip(t - win // 2, 0, s)
        hi = jnp.clip(t + win // 2, 0, s)
        csg = cs[..., g * POOL_GROUP:(g + 1) * POOL_GROUP]
        mean = (jnp.take(csg, hi, axis=1) - jnp.take(csg, lo, axis=1)) / (hi - lo).astype(jnp.float32)[None, :, None]
        diffs.append(mean.astype(h.dtype) - h[..., g * POOL_GROUP:(g + 1) * POOL_GROUP])
    dgrp = jnp.stack(diffs, axis=2)
    y = jnp.einsum('bsgc,gce->bsge', dgrp, pool_w).reshape(b, s, d)
    return y * pool_scale


def diff_attention(h, w_qkv, lam, subln_g, w_o, cos, sin, layer_idx):
    b, s, d = h.shape
    q, k, v = jnp.split(h @ w_qkv, 3, axis=-1)
    q = q.reshape(b, s, 2 * DIFF_HEADS, DIFF_HEAD_DIM)
    k = k.reshape(b, s, 2 * DIFF_HEADS, DIFF_HEAD_DIM)
    v = v.reshape(b, s, DIFF_HEADS, DIFF_V_DIM)
    q = jnp.concatenate([rope(q[..., :ROT_DIM], cos, sin), q[..., ROT_DIM:]], axis=-1)
    k = jnp.concatenate([rope(k[..., :ROT_DIM], cos, sin), k[..., ROT_DIM:]], axis=-1)
    q = (q * DIFF_HEAD_DIM ** -0.5).reshape(b, s, DIFF_HEADS, 2, DIFF_HEAD_DIM)
    k = k.reshape(b, s, DIFF_HEADS, 2, DIFF_HEAD_DIM)
    lam_init = 0.8 - 0.6 * math.exp(-0.3 * layer_idx)
    lf = lam.astype(jnp.float32)
    lam_full = jnp.exp(jnp.sum(lf[0] * lf[1])) - jnp.exp(jnp.sum(lf[2] * lf[3])) + lam_init

    def block(qb):
        sc = jnp.einsum('bqhcd,bkhcd->bhcqk', qb, k, preferred_element_type=jnp.float32)
        p = jax.nn.softmax(sc, axis=-1)
        a = p[:, :, 0] - lam_full * p[:, :, 1]
        return jnp.einsum('bhqk,bkhe->bqhe', a.astype(v.dtype), v)

    o = sweep_query_blocks(block, q)
    o = rms_norm(o, subln_g) * (1 - lam_init)
    return o.reshape(b, s, d) @ w_o


def gqa_axial_attention(h, w_qkv, q_norm_g, k_norm_g, w_o, cos_r, sin_r, cos_c, sin_c):
    b, s, d = h.shape
    qkv = h @ w_qkv
    q = qkv[..., :GQA_Q_DIM].reshape(b, s, GQA_HEADS, GQA_HEAD_DIM)
    k = qkv[..., GQA_Q_DIM:GQA_Q_DIM + GQA_KV_DIM].reshape(b, s, GQA_KV_HEADS, GQA_HEAD_DIM)
    v = qkv[..., GQA_Q_DIM + GQA_KV_DIM:].reshape(b, s, GQA_KV_HEADS, GQA_HEAD_DIM)
    q = rms_norm(q, q_norm_g)
    k = rms_norm(k, k_norm_g)

    def axial(t):
        return jnp.concatenate([rope(t[..., :AXIAL_DIM], cos_r, sin_r),
                                rope(t[..., AXIAL_DIM:], cos_c, sin_c)], axis=-1)

    q = (axial(q) * GQA_HEAD_DIM ** -0.5).reshape(b, s, GQA_KV_HEADS, GQA_GROUP, GQA_HEAD_DIM)
    k = axial(k)

    def block(qb):
        sc = jnp.einsum('bqgrd,bkgd->bgrqk', qb, k, preferred_element_type=jnp.float32)
        p = jax.nn.softmax(sc, axis=-1)
        return jnp.einsum('bgrqk,bkgd->bqgrd', p.astype(v.dtype), v)

    o = sweep_query_blocks(block, q)
    return o.reshape(b, s, d) @ w_o


def short_conv_mixer(h, w_in, w_conv, w_out):
    d = h.shape[-1]
    gb, gc, u = jnp.split(h @ w_in, 3, axis=-1)
    z = gc * u
    zc = lax.conv_general_dilated(z, w_conv[:, None, :], window_strides=(1,),
                                  padding=[((CONV_WIDTH - 1) // 2, (CONV_WIDTH - 1) // 2)],
                                  dimension_numbers=('NWC', 'WIO', 'NWC'),
                                  feature_group_count=d)
    return (gb * zc) @ w_out


def setup_inputs(seed: int = 0) -> dict:
    key = jax.random.key(seed)
    ks = jax.random.split(key, 24)

    def nrm(k, shape, scale):
        return jax.random.normal(k, shape, jnp.float32) * scale

    D = D_MODEL
    return {
        "x": nrm(ks[0], (BATCH, SEQ, D), 1.0),
        "c": nrm(ks[1], (BATCH, D), 1.0),
        "mod_w": nrm(ks[2], (DEPTH, D, N_MOD * D), 0.5 * D ** -0.5),
        "mod_b": nrm(ks[3], (DEPTH, N_MOD * D), 0.02),
        "norm_g": 1.0 + nrm(ks[4], (DEPTH, 3, D), 0.1),
        "ffn_w_gu": nrm(ks[5], (DEPTH, 2, D, 2 * D_FF), D ** -0.5),
        "ffn_w_down": nrm(ks[6], (DEPTH, 2, D_FF, D), D_FF ** -0.5),
        "pool_w": nrm(ks[7], (N_POOL, POOL_GROUP, POOL_GROUP), POOL_GROUP ** -0.5),
        "pool_scale": 1.0 + nrm(ks[8], (D,), 0.1),
        "diff_w_qkv": nrm(ks[9], (D, 3 * D), D ** -0.5),
        "diff_lambda": nrm(ks[10], (4, DIFF_HEAD_DIM), 0.1),
        "diff_subln_g": 1.0 + nrm(ks[11], (DIFF_V_DIM,), 0.1),
        "diff_w_o": nrm(ks[12], (D, D), D ** -0.5),
        "gqa_w_qkv": nrm(ks[13], (D, GQA_Q_DIM + 2 * GQA_KV_DIM), D ** -0.5),
        "gqa_q_norm_g": 1.0 + nrm(ks[14], (GQA_HEAD_DIM,), 0.1),
        "gqa_k_norm_g": 1.0 + nrm(ks[15], (GQA_HEAD_DIM,), 0.1),
        "gqa_w_o": nrm(ks[16], (D, D), D ** -0.5),
        "conv_w_in": nrm(ks[17], (D, 3 * D), D ** -0.5),
        "conv_w": nrm(ks[18], (CONV_WIDTH, D), CONV_WIDTH ** -0.5),
        "conv_w_out": nrm(ks[19], (D, D), D ** -0.5),
        "final_g": 1.0 + nrm(ks[20], (D,), 0.1),
    }


def reference(x, c, mod_w, mod_b, norm_g, ffn_w_gu, ffn_w_down, pool_w, pool_scale,
              diff_w_qkv, diff_lambda, diff_subln_g, diff_w_o,
              gqa_w_qkv, gqa_q_norm_g, gqa_k_norm_g, gqa_w_o,
              conv_w_in, conv_w, conv_w_out, final_g):
    s = x.shape[1]
    t = jnp.arange(s)
    cos1, sin1 = rope_tables(t, ROT_DIM, ROPE_THETA)
    rows = s // GRID_W
    row_pos = jnp.broadcast_to(jnp.arange(rows)[:, None], (rows, GRID_W)).reshape(-1)
    col_pos = jnp.broadcast_to(jnp.arange(GRID_W)[None, :], (rows, GRID_W)).reshape(-1)
    cos_r, sin_r = rope_tables(row_pos, AXIAL_DIM, AXIAL_THETA)
    cos_c, sin_c = rope_tables(col_pos, AXIAL_DIM, AXIAL_THETA)

    c_act = jax.nn.silu(c)
    for i in range(DEPTH):
        mod = (c_act @ mod_w[i] + mod_b[i])[:, None, :]
        sh0, sc0, g0, sh1, sc1, g1, sh2, sc2, g2 = jnp.split(mod, N_MOD, axis=-1)

        h = modulate(rms_norm(x, norm_g[i, 0]), sh0, sc0)
        x = x + 0.5 * g0 * swiglu(h, ffn_w_gu[i, 0], ffn_w_down[i, 0])

        h = modulate(rms_norm(x, norm_g[i, 1]), sh1, sc1)
        kind = i % N_MIXERS
        if kind == 0:
            y = pool_mixer(h, pool_w, pool_scale)
        elif kind == 1:
            y = diff_attention(h, diff_w_qkv, diff_lambda, diff_subln_g, diff_w_o, cos1, sin1, i)
        elif kind == 2:
            y = gqa_axial_attention(h, gqa_w_qkv, gqa_q_norm_g, gqa_k_norm_g, gqa_w_o,
                                    cos_r, sin_r, cos_c, sin_c)
        else:
            y = short_conv_mixer(h, conv_w_in, conv_w, conv_w_out)
        x = x + g1 * y

        h = modulate(rms_norm(x, norm_g[i, 2]), sh2, sc2)
        x = x + 0.5 * g2 * swiglu(h, ffn_w_gu[i, 1], ffn_w_down[i, 1])

    return rms_norm(x, final_g)
```

```python
import functools
import math

import jax
import jax.numpy as jnp
from jax import lax
from jax.experimental import pallas as pl
from jax.experimental.pallas import tpu as pltpu

F32 = jnp.float32
BF16 = jnp.bfloat16

LANES = 128
SUBLANES = 8
VMEM_LIMIT_BYTES = 56 * 1024 * 1024

EPS = 1e-6
GRID_W = 64
POOL_WINDOWS = (2, 4, 8, 16)
DIFF_HEADS = 8
ROPE_THETA = 500000.0
GQA_HEADS = 8
GQA_KV_HEADS = 2
AXIAL_THETA = 10000.0
N_MOD = 9

TOKEN_TILE = 512
FFN_CHUNK = 256
Q_TILE = 512
GQA_Q_TILE = 256
KV_TILE = 512
HALO = SUBLANES


def _params(*sem):
    return pltpu.CompilerParams(dimension_semantics=sem, vmem_limit_bytes=VMEM_LIMIT_BYTES)


def _silu(x):
    return x * jax.nn.sigmoid(x)


def _norm_mod(x, g, m_ref):
    ms = jnp.mean(x * x, axis=-1, keepdims=True)
    y = x * lax.rsqrt(ms + EPS) * g
    return y * (1.0 + m_ref[1:2, :]) + m_ref[0:1, :]


def _dot(a, b):
    return jnp.dot(a, b, preferred_element_type=F32)


def _mod_kernel(c_ref, w_ref, b_ref, o_ref):
    ca = _silu(c_ref[...]).astype(BF16)
    o_ref[...] = _dot(ca, w_ref[...].astype(BF16)) + b_ref[...]


def _mod_call(c, mod_w, mod_b):
    depth, d, nd = mod_w.shape
    b = c.shape[0]
    tn = d
    return pl.pallas_call(
        _mod_kernel,
        grid=(depth, nd // tn),
        in_specs=[
            pl.BlockSpec((b, d), lambda l, j: (0, 0)),
            pl.BlockSpec((None, d, tn), lambda l, j: (l, 0, j)),
            pl.BlockSpec((None, 1, tn), lambda l, j: (l, 0, j)),
        ],
        out_specs=pl.BlockSpec((None, b, tn), lambda l, j: (l, 0, j)),
        out_shape=jax.ShapeDtypeStruct((depth, b, nd), F32),
        compiler_params=_params("parallel", "parallel"),
        name="adaln_mod",
    )(c, mod_w, mod_b.reshape(depth, 1, nd))


def _ffn_kernel(x_ref, m_ref, g_ref, wgu_ref, wd_ref, o_ref, act_ref, *, d_ff, chunk):
    x = x_ref[...]
    h = _norm_mod(x, g_ref[...], m_ref).astype(BF16)
    for c0 in range(0, d_ff, chunk):
        gate = _dot(h, wgu_ref[:, c0:c0 + chunk])
        up = _dot(h, wgu_ref[:, d_ff + c0:d_ff + c0 + chunk])
        act_ref[:, c0:c0 + chunk] = (_silu(gate) * up).astype(BF16)
    y = _dot(act_ref[...], wd_ref[...])
    o_ref[...] = x + (0.5 * m_ref[2:3, :]) * y


def _ffn_call(x, m3, g, w_gu, w_down):
    b, s, d = x.shape
    d_ff = w_down.shape[0]
    tm = min(TOKEN_TILE, s)
    chunk = FFN_CHUNK if d_ff % FFN_CHUNK == 0 else d_ff
    kern = functools.partial(_ffn_kernel, d_ff=d_ff, chunk=chunk)
    return pl.pallas_call(
        kern,
        grid=(b, s // tm),
        in_specs=[
            pl.BlockSpec((None, tm, d), lambda bi, i: (bi, i, 0)),
            pl.BlockSpec((None, 3, d), lambda bi, i: (bi, 0, 0)),
            pl.BlockSpec((1, d), lambda bi, i: (0, 0)),
            pl.BlockSpec((d, 2 * d_ff), lambda bi, i: (0, 0)),
            pl.BlockSpec((d_ff, d), lambda bi, i: (0, 0)),
        ],
        out_specs=pl.BlockSpec((None, tm, d), lambda bi, i: (bi, i, 0)),
        out_shape=jax.ShapeDtypeStruct((b, s, d), F32),
        scratch_shapes=[pltpu.VMEM((tm, d_ff), BF16)],
        compiler_params=_params("parallel", "parallel"),
        name="ffn",
    )(x, m3, g.reshape(1, d), w_gu, w_down)


def _halo_specs(tm, d, s):
    per = tm // HALO
    last = s // HALO - 1
    return [
        pl.BlockSpec((None, tm, d), lambda bi, i: (bi, i, 0)),
        pl.BlockSpec((None, HALO, d), lambda bi, i: (bi, jnp.maximum(i * per - 1, 0), 0)),
        pl.BlockSpec((None, HALO, d), lambda bi, i: (bi, jnp.minimum((i + 1) * per, last), 0)),
    ]


def _fill_halo_scratch(hs_ref, x_ref, xp_ref, xn_ref, m_ref, g, tm):
    i = pl.program_id(1)
    n = pl.num_programs(1)
    h = _norm_mod(x_ref[...], g, m_ref)
    hp = _norm_mod(xp_ref[...], g, m_ref)
    hn = _norm_mod(xn_ref[...], g, m_ref)
    hs_ref[0:HALO, :] = jnp.where(i > 0, hp, 0.0)
    hs_ref[HALO:HALO + tm, :] = h
    hs_ref[HALO + tm:HALO + tm + HALO, :] = jnp.where(i < n - 1, hn, 0.0)
    return h


def _pool_kernel(x_ref, xp_ref, xn_ref, m_ref, g_ref, pw_ref, ps_ref, o_ref, hs_ref, *, seq, tm, group):
    i = pl.program_id(1)
    h = _fill_halo_scratch(hs_ref, x_ref, xp_ref, xn_ref, m_ref, g_ref[...], tm)
    t_abs = i * tm + lax.broadcasted_iota(jnp.int32, (tm, 1), 0)
    for gi, win in enumerate(POOL_WINDOWS):
        half = win // 2
        c0, c1 = gi * group, (gi + 1) * group
        acc = hs_ref[HALO - half:HALO - half + tm, c0:c1]
        for k in range(-half + 1, half):
            acc = acc + hs_ref[HALO + k:HALO + k + tm, c0:c1]
        lo = jnp.maximum(t_abs - half, 0)
        hi = jnp.minimum(t_abs + half, seq)
        cnt = (hi - lo).astype(F32)
        diff = acc / cnt - h[:, c0:c1]
        y = _dot(diff.astype(BF16), pw_ref[gi]) * ps_ref[:, c0:c1]
        o_ref[:, c0:c1] = x_ref[:, c0:c1] + m_ref[2:3, c0:c1] * y


def _pool_call(x, m3, g, pool_w, pool_scale):
    b, s, d = x.shape
    n_groups, group, _ = pool_w.shape
    assert max(POOL_WINDOWS) // 2 <= HALO
    tm = min(TOKEN_TILE, s)
    kern = functools.partial(_pool_kernel, seq=s, tm=tm, group=group)
    return pl.pallas_call(
        kern,
        grid=(b, s // tm),
        in_specs=_halo_specs(tm, d, s) + [
            pl.BlockSpec((None, 3, d), lambda bi, i: (bi, 0, 0)),
            pl.BlockSpec((1, d), lambda bi, i: (0, 0)),
            pl.BlockSpec((n_groups, group, group), lambda bi, i: (0, 0, 0)),
            pl.BlockSpec((1, d), lambda bi, i: (0, 0)),
        ],
        out_specs=pl.BlockSpec((None, tm, d), lambda bi, i: (bi, i, 0)),
        out_shape=jax.ShapeDtypeStruct((b, s, d), F32),
        scratch_shapes=[pltpu.VMEM((tm + 2 * HALO, d), F32)],
        compiler_params=_params("parallel", "parallel"),
        name="pool_mixer",
    )(x, x, x, m3, g.reshape(1, d), pool_w, pool_scale.reshape(1, d))


def _conv_kernel(x_ref, xp_ref, xn_ref, m_ref, g_ref, win_ref, cw_ref, wout_ref, o_ref, hs_ref, zs_ref, *, tm, d):
    _fill_halo_scratch(hs_ref, x_ref, xp_ref, xn_ref, m_ref, g_ref[...], tm)
    proj = _dot(hs_ref[...].astype(BF16), win_ref[...])
    gb = proj[HALO:HALO + tm, 0:d]
    zs_ref[...] = proj[:, d:2 * d] * proj[:, 2 * d:3 * d]
    zc = (cw_ref[0:1, :] * zs_ref[HALO - 1:HALO - 1 + tm, :]
          + cw_ref[1:2, :] * zs_ref[HALO:HALO + tm, :]
          + cw_ref[2:3, :] * zs_ref[HALO + 1:HALO + 1 + tm, :])
    y = _dot((gb * zc).astype(BF16), wout_ref[...])
    o_ref[...] = x_ref[...] + m_ref[2:3, :] * y


def _conv_call(x, m3, g, w_in, conv_w, w_out):
    b, s, d = x.shape
    assert conv_w.shape[0] == 3
    tm = min(TOKEN_TILE, s)
    kern = functools.partial(_conv_kernel, tm=tm, d=d)
    return pl.pallas_call(
        kern,
        grid=(b, s // tm),
        in_specs=_halo_specs(tm, d, s) + [
            pl.BlockSpec((None, 3, d), lambda bi, i: (bi, 0, 0)),
            pl.BlockSpec((1, d), lambda bi, i: (0, 0)),
            pl.BlockSpec((d, 3 * d), lambda bi, i: (0, 0)),
            pl.BlockSpec((3, d), lambda bi, i: (0, 0)),
            pl.BlockSpec((d, d), lambda bi, i: (0, 0)),
        ],
        out_specs=pl.BlockSpec((None, tm, d), lambda bi, i: (bi, i, 0)),
        out_shape=jax.ShapeDtypeStruct((b, s, d), F32),
        scratch_shapes=[pltpu.VMEM((tm + 2 * HALO, d), F32), pltpu.VMEM((tm + 2 * HALO, d), F32)],
        compiler_params=_params("parallel", "parallel"),
        name="conv_mixer",
    )(x, x, x, m3, g.reshape(1, d), w_in, conv_w, w_out)


def _rope_angles(pos, dim, theta):
    inv = 1.0 / (theta ** (jnp.arange(0, dim, 2, dtype=F32) / dim))
    ang = pos.astype(F32)[:, None] * inv[None, :]
    return jnp.cos(ang), jnp.sin(ang)


def _diff_rope_tables(s, head_dim, rot_dim):
    cos, sin = _rope_angles(jnp.arange(s), rot_dim, ROPE_THETA)
    half = rot_dim // 2
    ones = jnp.ones((s, head_dim - rot_dim), F32)
    zeros = jnp.zeros((s, head_dim - rot_dim), F32)
    zh = jnp.zeros((s, half), F32)
    c = jnp.concatenate([cos, cos, ones], axis=1)
    s1 = jnp.concatenate([zh, sin, zeros], axis=1)
    s2 = jnp.concatenate([-sin, zh, zeros], axis=1)
    reps = LANES // head_dim
    return tuple(jnp.tile(t, (1, reps)) for t in (c, s1, s2))


def _axial_rope_tables(s, axial_dim):
    rows = s // GRID_W
    row_pos = jnp.broadcast_to(jnp.arange(rows)[:, None], (rows, GRID_W)).reshape(-1)
    col_pos = jnp.broadcast_to(jnp.arange(GRID_W)[None, :], (rows, GRID_W)).reshape(-1)
    cr, sr = _rope_angles(row_pos, axial_dim, AXIAL_THETA)
    cc, sc = _rope_angles(col_pos, axial_dim, AXIAL_THETA)
    z = jnp.zeros_like(sr)
    c = jnp.concatenate([cr, cr, cc, cc], axis=1)
    s1 = jnp.concatenate([z, sr, z, sc], axis=1)
    s2 = jnp.concatenate([-sr, z, -sc, z], axis=1)
    return c, s1, s2


def _rotate(t, shift, c, s1, s2):
    return t * c + pltpu.roll(t, shift, 1) * s1 + pltpu.roll(t, LANES - shift, 1) * s2


def _diff_qkv_kernel(x_ref, m_ref, g_ref, w_ref, c_ref, s1_ref, s2_ref, q_ref, k_ref, v_ref, *, d, shift, scale):
    h = _norm_mod(x_ref[...], g_ref[...], m_ref).astype(BF16)
    proj = _dot(h, w_ref[...])
    c, s1, s2 = c_ref[...], s1_ref[...], s2_ref[...]
    for j in range(d // LANES):
        lo, hi = j * LANES, (j + 1) * LANES
        q_ref[:, lo:hi] = (_rotate(proj[:, lo:hi], shift, c, s1, s2) * scale).astype(BF16)
        k_ref[:, lo:hi] = _rotate(proj[:, d + lo:d + hi], shift, c, s1, s2).astype(BF16)
    v_ref[...] = proj[:, 2 * d:3 * d].astype(BF16)


def _diff_qkv_call(x, m3, g, w_qkv, tables, head_dim, rot_dim):
    b, s, d = x.shape
    tm = min(TOKEN_TILE, s)
    kern = functools.partial(_diff_qkv_kernel, d=d, shift=rot_dim // 2, scale=head_dim ** -0.5)
    tok = pl.BlockSpec((None, tm, d), lambda bi, i: (bi, i, 0))
    tab = pl.BlockSpec((tm, LANES), lambda bi, i: (i, 0))
    out = jax.ShapeDtypeStruct((b, s, d), BF16)
    return pl.pallas_call(
        kern,
        grid=(b, s // tm),
        in_specs=[
            tok,
            pl.BlockSpec((None, 3, d), lambda bi, i: (bi, 0, 0)),
            pl.BlockSpec((1, d), lambda bi, i: (0, 0)),
            pl.BlockSpec((d, 3 * d), lambda bi, i: (0, 0)),
            tab, tab, tab,
        ],
        out_specs=[tok, tok, tok],
        out_shape=[out, out, out],
        compiler_params=_params("parallel", "parallel"),
        name="diff_qkv",
    )(x, m3, g.reshape(1, d), w_qkv, *tables)


def _gqa_qkv_kernel(x_ref, m_ref, g_ref, w_ref, qg_ref, kg_ref, c_ref, s1_ref, s2_ref,
                    q_ref, k_ref, v_ref, *, n_q, n_kv, shift, scale):
    h = _norm_mod(x_ref[...], g_ref[...], m_ref).astype(BF16)
    proj = _dot(h, w_ref[...])
    c, s1, s2 = c_ref[...], s1_ref[...], s2_ref[...]

    def head(j, gain):
        t = proj[:, j * LANES:(j + 1) * LANES]
        ms = jnp.mean(t * t, axis=-1, keepdims=True)
        return _rotate(t * lax.rsqrt(ms + EPS) * gain, shift, c, s1, s2)

    for j in range(n_q):
        q_ref[:, j * LANES:(j + 1) * LANES] = (head(j, qg_ref[...]) * scale).astype(BF16)
    for j in range(n_kv):
        k_ref[:, j * LANES:(j + 1) * LANES] = head(n_q + j, kg_ref[...]).astype(BF16)
    v_ref[...] = proj[:, (n_q + n_kv) * LANES:].astype(BF16)


def _gqa_qkv_call(x, m3, g, w_qkv, q_gain, k_gain, tables):
    b, s, d = x.shape
    hd = q_gain.shape[0]
    assert hd == LANES
    n_q, n_kv = GQA_HEADS, GQA_KV_HEADS
    tm = min(TOKEN_TILE, s)
    kern = functools.partial(_gqa_qkv_kernel, n_q=n_q, n_kv=n_kv, shift=hd // 4, scale=hd ** -0.5)
    tab = pl.BlockSpec((tm, LANES), lambda bi, i: (i, 0))

    def tok(width):
        return pl.BlockSpec((None, tm, width), lambda bi, i: (bi, i, 0))

    return pl.pallas_call(
        kern,
        grid=(b, s // tm),
        in_specs=[
            tok(d),
            pl.BlockSpec((None, 3, d), lambda bi, i: (bi, 0, 0)),
            pl.BlockSpec((1, d), lambda bi, i: (0, 0)),
            pl.BlockSpec(w_qkv.shape, lambda bi, i: (0, 0)),
            pl.BlockSpec((1, hd), lambda bi, i: (0, 0)),
            pl.BlockSpec((1, hd), lambda bi, i: (0, 0)),
            tab, tab, tab,
        ],
        out_specs=[tok(n_q * hd), tok(n_kv * hd), tok(n_kv * hd)],
        out_shape=[jax.ShapeDtypeStruct((b, s, n_q * hd), BF16),
                   jax.ShapeDtypeStruct((b, s, n_kv * hd), BF16),
                   jax.ShapeDtypeStruct((b, s, n_kv * hd), BF16)],
        compiler_params=_params("parallel", "parallel"),
        name="gqa_qkv",
    )(x, m3, g.reshape(1, d), w_qkv, q_gain.reshape(1, hd), k_gain.reshape(1, hd), *tables)


def _flash_sweep(qs_ref, k_ref, v_ref, m_ref, l_ref, acc_ref, *, seq, tk):
    m_ref[...] = jnp.full(m_ref.shape, -jnp.inf, F32)
    l_ref[...] = jnp.zeros(l_ref.shape, F32)
    acc_ref[...] = jnp.zeros(acc_ref.shape, F32)
    rows = qs_ref.shape[0]
    n_chunks = tk // LANES

    def step(j, carry):
        start = pl.multiple_of(j * tk, tk)
        k = k_ref[pl.ds(start, tk), :]
        v = v_ref[pl.ds(start, tk), :]
        s = lax.dot_general(qs_ref[...], k, (((1,), (1,)), ((), ())), preferred_element_type=F32)
        chunks = [s[:, c * LANES:(c + 1) * LANES] for c in range(n_chunks)]
        m_lane = functools.reduce(jnp.maximum, chunks)
        m_prev = m_ref[...]
        m_next = jnp.maximum(m_prev, jnp.max(m_lane, axis=1, keepdims=True))
        p_chunks = [jnp.exp(ch - m_next) for ch in chunks]
        alpha = jnp.exp(m_prev - m_next)
        l_ref[...] = alpha * l_ref[...] + functools.reduce(jnp.add, p_chunks)
        p = jnp.concatenate([pc.astype(BF16) for pc in p_chunks], axis=1)
        acc_ref[...] = alpha * acc_ref[...] + _dot(p, v)
        m_ref[...] = m_next
        return carry

    lax.fori_loop(0, seq // tk, step, 0)
    l = jnp.sum(l_ref[...], axis=1, keepdims=True)
    del rows
    return acc_ref[...] / l


def _diff_flash_kernel(q_ref, k_ref, v_ref, lam_ref, sg_ref, o_ref, qs_ref, m_ref, l_ref, acc_ref,
                       *, seq, tq, tk, half, lam_init):
    q = q_ref[...]
    lane = lax.broadcasted_iota(jnp.int32, q.shape, 1)
    zero = jnp.zeros_like(q)
    qs_ref[0:tq, :] = jnp.where(lane < half, q, zero)
    qs_ref[tq:2 * tq, :] = jnp.where(lane >= half, q, zero)
    on = _flash_sweep(qs_ref, k_ref, v_ref, m_ref, l_ref, acc_ref, seq=seq, tk=tk)
    lf = lam_ref[...]
    lam = (jnp.exp(jnp.sum(lf[0:1, :] * lf[1:2, :], axis=1, keepdims=True))
           - jnp.exp(jnp.sum(lf[2:3, :] * lf[3:4, :], axis=1, keepdims=True)) + lam_init)
    o = on[0:tq, :] - lam * on[tq:2 * tq, :]
    ms = jnp.mean(o * o, axis=-1, keepdims=True)
    o_ref[...] = ((o * lax.rsqrt(ms + EPS) * sg_ref[...]) * (1.0 - lam_init)).astype(BF16)


def _diff_flash_call(q, k, v, lam, subln_g, layer_idx):
    b, s, d = q.shape
    v_dim = subln_g.shape[0]
    assert v_dim == LANES and d == DIFF_HEADS * v_dim
    tq, tk = min(Q_TILE, s), min(KV_TILE, s)
    lam_init = 0.8 - 0.6 * math.exp(-0.3 * layer_idx)
    kern = functools.partial(_diff_flash_kernel, seq=s, tq=tq, tk=tk, half=v_dim // 2, lam_init=lam_init)
    kv = pl.BlockSpec((None, s, v_dim), lambda bi, h, i: (bi, 0, h))
    qo = pl.BlockSpec((None, tq, v_dim), lambda bi, h, i: (bi, i, h))
    return pl.pallas_call(
        kern,
        grid=(b, DIFF_HEADS, s // tq),
        in_specs=[qo, kv, kv,
                  pl.BlockSpec(lam.shape, lambda bi, h, i: (0, 0)),
                  pl.BlockSpec((1, v_dim), lambda bi, h, i: (0, 0))],
        out_specs=qo,
        out_shape=jax.ShapeDtypeStruct((b, s, d), BF16),
        scratch_shapes=[pltpu.VMEM((2 * tq, v_dim), BF16),
                        pltpu.VMEM((2 * tq, LANES), F32),
                        pltpu.VMEM((2 * tq, LANES), F32),
                        pltpu.VMEM((2 * tq, v_dim), F32)],
        compiler_params=_params("parallel", "parallel", "parallel"),
        name="diff_flash",
    )(q, k, v, lam, subln_g.reshape(1, v_dim))


def _gqa_flash_kernel(q_ref, k_ref, v_ref, o_ref, qs_ref, m_ref, l_ref, acc_ref, *, seq, tq, tk, group):
    for r in range(group):
        qs_ref[r * tq:(r + 1) * tq, :] = q_ref[:, r * LANES:(r + 1) * LANES]
    on = _flash_sweep(qs_ref, k_ref, v_ref, m_ref, l_ref, acc_ref, seq=seq, tk=tk)
    for r in range(group):
        o_ref[:, r * LANES:(r + 1) * LANES] = on[r * tq:(r + 1) * tq, :].astype(BF16)


def _gqa_flash_call(q, k, v):
    b, s, d = q.shape
    hd = LANES
    group = GQA_HEADS // GQA_KV_HEADS
    tq, tk = min(GQA_Q_TILE, s), min(KV_TILE, s)
    kern = functools.partial(_gqa_flash_kernel, seq=s, tq=tq, tk=tk, group=group)
    kv = pl.BlockSpec((None, s, hd), lambda bi, h, i: (bi, 0, h))
    qo = pl.BlockSpec((None, tq, group * hd), lambda bi, h, i: (bi, i, h))
    return pl.pallas_call(
        kern,
        grid=(b, GQA_KV_HEADS, s // tq),
        in_specs=[qo, kv, kv],
        out_specs=qo,
        out_shape=jax.ShapeDtypeStruct((b, s, d), BF16),
        scratch_shapes=[pltpu.VMEM((group * tq, hd), BF16),
                        pltpu.VMEM((group * tq, LANES), F32),
                        pltpu.VMEM((group * tq, LANES), F32),
                        pltpu.VMEM((group * tq, hd), F32)],
        compiler_params=_params("parallel", "parallel", "parallel"),
        name="gqa_flash",
    )(q, k, v)


def _oproj_kernel(x_ref, a_ref, m_ref, w_ref, o_ref):
    o_ref[...] = x_ref[...] + m_ref[2:3, :] * _dot(a_ref[...], w_ref[...])


def _oproj_call(x, a, m3, w_o):
    b, s, d = x.shape
    tm = min(TOKEN_TILE, s)
    tok = pl.BlockSpec((None, tm, d), lambda bi, i: (bi, i, 0))
    return pl.pallas_call(
        _oproj_kernel,
        grid=(b, s // tm),
        in_specs=[tok, tok,
                  pl.BlockSpec((None, 3, d), lambda bi, i: (bi, 0, 0)),
                  pl.BlockSpec((d, d), lambda bi, i: (0, 0))],
        out_specs=tok,
        out_shape=jax.ShapeDtypeStruct((b, s, d), F32),
        compiler_params=_params("parallel", "parallel"),
        name="attn_oproj",
    )(x, a, m3, w_o)


def _final_norm_kernel(x_ref, g_ref, o_ref):
    x = x_ref[...]
    ms = jnp.mean(x * x, axis=-1, keepdims=True)
    o_ref[...] = x * lax.rsqrt(ms + EPS) * g_ref[...]


def _final_norm_call(x, g):
    b, s, d = x.shape
    tm = min(TOKEN_TILE, s)
    tok = pl.BlockSpec((None, tm, d), lambda bi, i: (bi, i, 0))
    return pl.pallas_call(
        _final_norm_kernel,
        grid=(b, s // tm),
        in_specs=[tok, pl.BlockSpec((1, d), lambda bi, i: (0, 0))],
        out_specs=tok,
        out_shape=jax.ShapeDtypeStruct((b, s, d), F32),
        compiler_params=_params("parallel", "parallel"),
        name="final_norm",
    )(x, g.reshape(1, d))


def kernel(x, c, mod_w, mod_b, norm_g, ffn_w_gu, ffn_w_down, pool_w, pool_scale,
           diff_w_qkv, diff_lambda, diff_subln_g, diff_w_o,
           gqa_w_qkv, gqa_q_norm_g, gqa_k_norm_g, gqa_w_o,
           conv_w_in, conv_w, conv_w_out, final_g):
    b, s, d = x.shape
    depth = mod_w.shape[0]
    diff_head_dim = diff_lambda.shape[1]
    diff_tables = _diff_rope_tables(s, diff_head_dim, diff_head_dim // 4)
    axial_tables = _axial_rope_tables(s, gqa_q_norm_g.shape[0] // 2)

    mods = _mod_call(c, mod_w, mod_b).reshape(depth, b, N_MOD, d)
    w_gu, w_down = ffn_w_gu.astype(BF16), ffn_w_down.astype(BF16)

    for i in range(depth):
        m_ffn0, m_mix, m_ffn1 = (mods[i, :, 3 * k:3 * k + 3, :] for k in range(3))
        x = _ffn_call(x, m_ffn0, norm_g[i, 0], w_gu[i, 0], w_down[i, 0])
        kind = i % 4
        if kind == 0:
            x = _pool_call(x, m_mix, norm_g[i, 1], pool_w.astype(BF16), pool_scale)
        elif kind == 1:
            q, k, v = _diff_qkv_call(x, m_mix, norm_g[i, 1], diff_w_qkv.astype(BF16), diff_tables,
                                     diff_head_dim, diff_head_dim // 4)
            a = _diff_flash_call(q, k, v, diff_lambda, diff_subln_g, i)
            x = _oproj_call(x, a, m_mix, diff_w_o.astype(BF16))
        elif kind == 2:
            q, k, v = _gqa_qkv_call(x, m_mix, norm_g[i, 1], gqa_w_qkv.astype(BF16),
                                    gqa_q_norm_g, gqa_k_norm_g, axial_tables)
            a = _gqa_flash_call(q, k, v)
            x = _oproj_call(x, a, m_mix, gqa_w_o.astype(BF16))
        else:
            x = _conv_call(x, m_mix, norm_g[i, 1], conv_w_in.astype(BF16), conv_w, conv_w_out.astype(BF16))
        x = _ffn_call(x, m_ffn1, norm_g[i, 2], w_gu[i, 1], w_down[i, 1])

    return _final_norm_call(x, final_g)
```

```python
import functools
import math

import jax
import jax.numpy as jnp
from jax import lax
from jax.experimental import pallas as pl
from jax.experimental.pallas import tpu as pltpu

F32 = jnp.float32
BF16 = jnp.bfloat16

LANES = 128
SUBLANES = 8
VMEM_LIMIT_BYTES = 56 * 1024 * 1024

EPS = 1e-6
LOG2_E = math.log2(math.e)
GRID_W = 64
POOL_WINDOWS = (2, 4, 8, 16)
DIFF_HEADS = 8
ROPE_THETA = 500000.0
GQA_HEADS = 8
GQA_KV_HEADS = 2
AXIAL_THETA = 10000.0
N_MOD = 9

TOKEN_TILE = 512
FFN_CHUNK = 256
Q_TILE = 512
GQA_Q_TILE = 256
KV_TILE = 1024
HALO = SUBLANES


def _params(*sem):
    return pltpu.CompilerParams(dimension_semantics=sem, vmem_limit_bytes=VMEM_LIMIT_BYTES)


def _silu(x):
    return x * jax.nn.sigmoid(x)


def _norm_mod(x, g, m_ref):
    ms = jnp.mean(x * x, axis=-1, keepdims=True)
    y = x * lax.rsqrt(ms + EPS) * g
    return y * (1.0 + m_ref[1:2, :]) + m_ref[0:1, :]


def _dot(a, b):
    return jnp.dot(a, b, preferred_element_type=F32)


def _mod_kernel(c_ref, w_ref, b_ref, o_ref):
    ca = _silu(c_ref[...]).astype(BF16)
    o_ref[...] = _dot(ca, w_ref[...].astype(BF16)) + b_ref[...]


def _mod_call(c, mod_w, mod_b):
    depth, d, nd = mod_w.shape
    b = c.shape[0]
    tn = d
    return pl.pallas_call(
        _mod_kernel,
        grid=(depth, nd // tn),
        in_specs=[
            pl.BlockSpec((b, d), lambda l, j: (0, 0)),
            pl.BlockSpec((None, d, tn), lambda l, j: (l, 0, j)),
            pl.BlockSpec((None, 1, tn), lambda l, j: (l, 0, j)),
        ],
        out_specs=pl.BlockSpec((None, b, tn), lambda l, j: (l, 0, j)),
        out_shape=jax.ShapeDtypeStruct((depth, b, nd), F32),
        compiler_params=_params("parallel", "parallel"),
        name="adaln_mod",
    )(c, mod_w, mod_b.reshape(depth, 1, nd))


def _ffn_kernel(x_ref, m_ref, g_ref, wgu_ref, wd_ref, o_ref, act_ref, *, d_ff, chunk):
    x = x_ref[...]
    h = _norm_mod(x, g_ref[...], m_ref).astype(BF16)
    for c0 in range(0, d_ff, chunk):
        gate = _dot(h, wgu_ref[:, c0:c0 + chunk])
        up = _dot(h, wgu_ref[:, d_ff + c0:d_ff + c0 + chunk])
        act_ref[:, c0:c0 + chunk] = (_silu(gate) * up).astype(BF16)
    y = _dot(act_ref[...], wd_ref[...])
    o_ref[...] = x + (0.5 * m_ref[2:3, :]) * y


def _ffn_call(x, m3, g, w_gu, w_down):
    b, s, d = x.shape
    d_ff = w_down.shape[0]
    tm = min(TOKEN_TILE, s)
    chunk = FFN_CHUNK if d_ff % FFN_CHUNK == 0 else d_ff
    kern = functools.partial(_ffn_kernel, d_ff=d_ff, chunk=chunk)
    return pl.pallas_call(
        kern,
        grid=(b, s // tm),
        in_specs=[
            pl.BlockSpec((None, tm, d), lambda bi, i: (bi, i, 0)),
            pl.BlockSpec((None, 3, d), lambda bi, i: (bi, 0, 0)),
            pl.BlockSpec((1, d), lambda bi, i: (0, 0)),
            pl.BlockSpec((d, 2 * d_ff), lambda bi, i: (0, 0)),
            pl.BlockSpec((d_ff, d), lambda bi, i: (0, 0)),
        ],
        out_specs=pl.BlockSpec((None, tm, d), lambda bi, i: (bi, i, 0)),
        out_shape=jax.ShapeDtypeStruct((b, s, d), F32),
        scratch_shapes=[pltpu.VMEM((tm, d_ff), BF16)],
        compiler_params=_params("parallel", "parallel"),
        name="ffn",
    )(x, m3, g.reshape(1, d), w_gu, w_down)


def _halo_specs(tm, d, s):
    per = tm // HALO
    last = s // HALO - 1
    return [
        pl.BlockSpec((None, tm, d), lambda bi, i: (bi, i, 0)),
        pl.BlockSpec((None, HALO, d), lambda bi, i: (bi, jnp.maximum(i * per - 1, 0), 0)),
        pl.BlockSpec((None, HALO, d), lambda bi, i: (bi, jnp.minimum((i + 1) * per, last), 0)),
    ]


def _fill_halo_scratch(hs_ref, x_ref, xp_ref, xn_ref, m_ref, g, tm):
    i = pl.program_id(1)
    n = pl.num_programs(1)
    h = _norm_mod(x_ref[...], g, m_ref)
    hp = _norm_mod(xp_ref[...], g, m_ref)
    hn = _norm_mod(xn_ref[...], g, m_ref)
    hs_ref[0:HALO, :] = jnp.where(i > 0, hp, 0.0)
    hs_ref[HALO:HALO + tm, :] = h
    hs_ref[HALO + tm:HALO + tm + HALO, :] = jnp.where(i < n - 1, hn, 0.0)
    return h


def _pool_kernel(x_ref, xp_ref, xn_ref, m_ref, g_ref, pw_ref, ps_ref, o_ref, hs_ref, *, seq, tm, group):
    i = pl.program_id(1)
    h = _fill_halo_scratch(hs_ref, x_ref, xp_ref, xn_ref, m_ref, g_ref[...], tm)
    t_abs = i * tm + lax.broadcasted_iota(jnp.int32, (tm, 1), 0)
    for gi, win in enumerate(POOL_WINDOWS):
        half = win // 2
        c0, c1 = gi * group, (gi + 1) * group
        acc = hs_ref[HALO - half:HALO - half + tm, c0:c1]
        for k in range(-half + 1, half):
            acc = acc + hs_ref[HALO + k:HALO + k + tm, c0:c1]
        lo = jnp.maximum(t_abs - half, 0)
        hi = jnp.minimum(t_abs + half, seq)
        cnt = (hi - lo).astype(F32)
        diff = acc / cnt - h[:, c0:c1]
        y = _dot(diff.astype(BF16), pw_ref[gi]) * ps_ref[:, c0:c1]
        o_ref[:, c0:c1] = x_ref[:, c0:c1] + m_ref[2:3, c0:c1] * y


def _pool_call(x, m3, g, pool_w, pool_scale):
    b, s, d = x.shape
    n_groups, group, _ = pool_w.shape
    assert max(POOL_WINDOWS) // 2 <= HALO
    tm = min(TOKEN_TILE, s)
    kern = functools.partial(_pool_kernel, seq=s, tm=tm, group=group)
    return pl.pallas_call(
        kern,
        grid=(b, s // tm),
        in_specs=_halo_specs(tm, d, s) + [
            pl.BlockSpec((None, 3, d), lambda bi, i: (bi, 0, 0)),
            pl.BlockSpec((1, d), lambda bi, i: (0, 0)),
            pl.BlockSpec((n_groups, group, group), lambda bi, i: (0, 0, 0)),
            pl.BlockSpec((1, d), lambda bi, i: (0, 0)),
        ],
        out_specs=pl.BlockSpec((None, tm, d), lambda bi, i: (bi, i, 0)),
        out_shape=jax.ShapeDtypeStruct((b, s, d), F32),
        scratch_shapes=[pltpu.VMEM((tm + 2 * HALO, d), F32)],
        compiler_params=_params("parallel", "parallel"),
        name="pool_mixer",
    )(x, x, x, m3, g.reshape(1, d), pool_w, pool_scale.reshape(1, d))


def _conv_kernel(x_ref, xp_ref, xn_ref, m_ref, g_ref, win_ref, cw_ref, wout_ref, o_ref, hs_ref, zs_ref, *, tm, d):
    _fill_halo_scratch(hs_ref, x_ref, xp_ref, xn_ref, m_ref, g_ref[...], tm)
    proj = _dot(hs_ref[...].astype(BF16), win_ref[...])
    gb = proj[HALO:HALO + tm, 0:d]
    zs_ref[...] = proj[:, d:2 * d] * proj[:, 2 * d:3 * d]
    zc = (cw_ref[0:1, :] * zs_ref[HALO - 1:HALO - 1 + tm, :]
          + cw_ref[1:2, :] * zs_ref[HALO:HALO + tm, :]
          + cw_ref[2:3, :] * zs_ref[HALO + 1:HALO + 1 + tm, :])
    y = _dot((gb * zc).astype(BF16), wout_ref[...])
    o_ref[...] = x_ref[...] + m_ref[2:3, :] * y


def _conv_call(x, m3, g, w_in, conv_w, w_out):
    b, s, d = x.shape
    assert conv_w.shape[0] == 3
    tm = min(TOKEN_TILE, s)
    kern = functools.partial(_conv_kernel, tm=tm, d=d)
    return pl.pallas_call(
        kern,
        grid=(b, s // tm),
        in_specs=_halo_specs(tm, d, s) + [
            pl.BlockSpec((None, 3, d), lambda bi, i: (bi, 0, 0)),
            pl.BlockSpec((1, d), lambda bi, i: (0, 0)),
            pl.BlockSpec((d, 3 * d), lambda bi, i: (0, 0)),
            pl.BlockSpec((3, d), lambda bi, i: (0, 0)),
            pl.BlockSpec((d, d), lambda bi, i: (0, 0)),
        ],
        out_specs=pl.BlockSpec((None, tm, d), lambda bi, i: (bi, i, 0)),
        out_shape=jax.ShapeDtypeStruct((b, s, d), F32),
        scratch_shapes=[pltpu.VMEM((tm + 2 * HALO, d), F32), pltpu.VMEM((tm + 2 * HALO, d), F32)],
        compiler_params=_params("parallel", "parallel"),
        name="conv_mixer",
    )(x, x, x, m3, g.reshape(1, d), w_in, conv_w, w_out)


def _rope_angles(pos, dim, theta):
    inv = 1.0 / (theta ** (jnp.arange(0, dim, 2, dtype=F32) / dim))
    ang = pos.astype(F32)[:, None] * inv[None, :]
    return jnp.cos(ang), jnp.sin(ang)


def _diff_rope_tables(s, head_dim, rot_dim):
    cos, sin = _rope_angles(jnp.arange(s), rot_dim, ROPE_THETA)
    half = rot_dim // 2
    ones = jnp.ones((s, head_dim - rot_dim), F32)
    zeros = jnp.zeros((s, head_dim - rot_dim), F32)
    zh = jnp.zeros((s, half), F32)
    c = jnp.concatenate([cos, cos, ones], axis=1)
    s1 = jnp.concatenate([zh, sin, zeros], axis=1)
    s2 = jnp.concatenate([-sin, zh, zeros], axis=1)
    reps = LANES // head_dim
    return tuple(jnp.tile(t, (1, reps)) for t in (c, s1, s2))


def _axial_rope_tables(s, axial_dim):
    rows = s // GRID_W
    row_pos = jnp.broadcast_to(jnp.arange(rows)[:, None], (rows, GRID_W)).reshape(-1)
    col_pos = jnp.broadcast_to(jnp.arange(GRID_W)[None, :], (rows, GRID_W)).reshape(-1)
    cr, sr = _rope_angles(row_pos, axial_dim, AXIAL_THETA)
    cc, sc = _rope_angles(col_pos, axial_dim, AXIAL_THETA)
    z = jnp.zeros_like(sr)
    c = jnp.concatenate([cr, cr, cc, cc], axis=1)
    s1 = jnp.concatenate([z, sr, z, sc], axis=1)
    s2 = jnp.concatenate([-sr, z, -sc, z], axis=1)
    return c, s1, s2


def _rotate(t, shift, c, s1, s2):
    return t * c + pltpu.roll(t, shift, 1) * s1 + pltpu.roll(t, LANES - shift, 1) * s2


def _diff_qkv_kernel(x_ref, m_ref, g_ref, w_ref, c_ref, s1_ref, s2_ref, q_ref, k_ref, v_ref, *, d, shift, scale):
    h = _norm_mod(x_ref[...], g_ref[...], m_ref).astype(BF16)
    proj = _dot(h, w_ref[...])
    c, s1, s2 = c_ref[...], s1_ref[...], s2_ref[...]
    for j in range(d // LANES):
        lo, hi = j * LANES, (j + 1) * LANES
        q_ref[:, lo:hi] = (_rotate(proj[:, lo:hi], shift, c, s1, s2) * scale).astype(BF16)
        k_ref[:, lo:hi] = _rotate(proj[:, d + lo:d + hi], shift, c, s1, s2).astype(BF16)
    v_ref[...] = proj[:, 2 * d:3 * d].astype(BF16)


def _diff_qkv_call(x, m3, g, w_qkv, tables, head_dim, rot_dim):
    b, s, d = x.shape
    tm = min(TOKEN_TILE, s)
    kern = functools.partial(_diff_qkv_kernel, d=d, shift=rot_dim // 2, scale=head_dim ** -0.5 * LOG2_E)
    tok = pl.BlockSpec((None, tm, d), lambda bi, i: (bi, i, 0))
    tab = pl.BlockSpec((tm, LANES), lambda bi, i: (i, 0))
    out = jax.ShapeDtypeStruct((b, s, d), BF16)
    return pl.pallas_call(
        kern,
        grid=(b, s // tm),
        in_specs=[
            tok,
            pl.BlockSpec((None, 3, d), lambda bi, i: (bi, 0, 0)),
            pl.BlockSpec((1, d), lambda bi, i: (0, 0)),
            pl.BlockSpec((d, 3 * d), lambda bi, i: (0, 0)),
            tab, tab, tab,
        ],
        out_specs=[tok, tok, tok],
        out_shape=[out, out, out],
        compiler_params=_params("parallel", "parallel"),
        name="diff_qkv",
    )(x, m3, g.reshape(1, d), w_qkv, *tables)


def _gqa_qkv_kernel(x_ref, m_ref, g_ref, w_ref, qg_ref, kg_ref, c_ref, s1_ref, s2_ref,
                    q_ref, k_ref, v_ref, *, n_q, n_kv, shift, scale):
    h = _norm_mod(x_ref[...], g_ref[...], m_ref).astype(BF16)
    proj = _dot(h, w_ref[...])
    c, s1, s2 = c_ref[...], s1_ref[...], s2_ref[...]

    def head(j, gain):
        t = proj[:, j * LANES:(j + 1) * LANES]
        ms = jnp.mean(t * t, axis=-1, keepdims=True)
        return _rotate(t * lax.rsqrt(ms + EPS) * gain, shift, c, s1, s2)

    for j in range(n_q):
        q_ref[:, j * LANES:(j + 1) * LANES] = (head(j, qg_ref[...]) * scale).astype(BF16)
    for j in range(n_kv):
        k_ref[:, j * LANES:(j + 1) * LANES] = head(n_q + j, kg_ref[...]).astype(BF16)
    v_ref[...] = proj[:, (n_q + n_kv) * LANES:].astype(BF16)


def _gqa_qkv_call(x, m3, g, w_qkv, q_gain, k_gain, tables):
    b, s, d = x.shape
    hd = q_gain.shape[0]
    assert hd == LANES
    n_q, n_kv = GQA_HEADS, GQA_KV_HEADS
    tm = min(TOKEN_TILE, s)
    kern = functools.partial(_gqa_qkv_kernel, n_q=n_q, n_kv=n_kv, shift=hd // 4, scale=hd ** -0.5 * LOG2_E)
    tab = pl.BlockSpec((tm, LANES), lambda bi, i: (i, 0))

    def tok(width):
        return pl.BlockSpec((None, tm, width), lambda bi, i: (bi, i, 0))

    return pl.pallas_call(
        kern,
        grid=(b, s // tm),
        in_specs=[
            tok(d),
            pl.BlockSpec((None, 3, d), lambda bi, i: (bi, 0, 0)),
            pl.BlockSpec((1, d), lambda bi, i: (0, 0)),
            pl.BlockSpec(w_qkv.shape, lambda bi, i: (0, 0)),
            pl.BlockSpec((1, hd), lambda bi, i: (0, 0)),
            pl.BlockSpec((1, hd), lambda bi, i: (0, 0)),
            tab, tab, tab,
        ],
        out_specs=[tok(n_q * hd), tok(n_kv * hd), tok(n_kv * hd)],
        out_shape=[jax.ShapeDtypeStruct((b, s, n_q * hd), BF16),
                   jax.ShapeDtypeStruct((b, s, n_kv * hd), BF16),
                   jax.ShapeDtypeStruct((b, s, n_kv * hd), BF16)],
        compiler_params=_params("parallel", "parallel"),
        name="gqa_qkv",
    )(x, m3, g.reshape(1, d), w_qkv, q_gain.reshape(1, hd), k_gain.reshape(1, hd), *tables)


def _flash_sweep(qs_ref, k_ref, v_ref, p_ref, m_ref, l_ref, acc_ref, *, seq, tk):
    n_chunks = tk // LANES

    def scores(j):
        k = k_ref[pl.ds(pl.multiple_of(j * tk, tk), tk), :]
        s = lax.dot_general(qs_ref[...], k, (((1,), (1,)), ((), ())), preferred_element_type=F32)
        return [s[:, c * LANES:(c + 1) * LANES] for c in range(n_chunks)]

    def p_times_v(j):
        return _dot(p_ref[...], v_ref[pl.ds(pl.multiple_of(j * tk, tk), tk), :])

    def exponentials(chunks, m_next):
        p_chunks = [jnp.exp2(ch - m_next) for ch in chunks]
        p_ref[...] = jnp.concatenate([pc.astype(BF16) for pc in p_chunks], axis=1)
        return functools.reduce(jnp.add, p_chunks)

    chunks = scores(0)
    m0 = jnp.max(functools.reduce(jnp.maximum, chunks), axis=1, keepdims=True)
    m0 = jnp.broadcast_to(m0, m_ref.shape)
    m_ref[...] = m0
    l_ref[...] = exponentials(chunks, m0)
    acc_ref[...] = jnp.zeros(acc_ref.shape, F32)

    def step(j, carry):
        chunks = scores(j)
        pv = p_times_v(j - 1)
        m_prev = m_ref[...]
        m_lane = functools.reduce(jnp.maximum, chunks)
        m_next = jnp.maximum(m_prev, jnp.max(m_lane, axis=1, keepdims=True))
        alpha = jnp.exp2(m_prev - m_next)
        m_ref[...] = m_next
        l_ref[...] = alpha * l_ref[...] + exponentials(chunks, m_next)
        acc_ref[...] = alpha * (acc_ref[...] + pv)
        return carry

    n = seq // tk
    lax.fori_loop(1, n, step, 0)
    acc = acc_ref[...] + p_times_v(n - 1)
    return acc / jnp.sum(l_ref[...], axis=1, keepdims=True)


def _flash_scratch(rows, tk):
    return [pltpu.VMEM((rows, LANES), BF16),
            pltpu.VMEM((rows, tk), BF16),
            pltpu.VMEM((rows, LANES), F32),
            pltpu.VMEM((rows, LANES), F32),
            pltpu.VMEM((rows, LANES), F32)]


def _diff_flash_kernel(q_ref, k_ref, v_ref, lam_ref, sg_ref, o_ref, qs_ref, p_ref, m_ref, l_ref, acc_ref,
                       *, seq, tq, tk, half, lam_init):
    q = q_ref[...]
    lane = lax.broadcasted_iota(jnp.int32, q.shape, 1)
    zero = jnp.zeros_like(q)
    qs_ref[0:tq, :] = jnp.where(lane < half, q, zero)
    qs_ref[tq:2 * tq, :] = jnp.where(lane >= half, q, zero)
    on = _flash_sweep(qs_ref, k_ref, v_ref, p_ref, m_ref, l_ref, acc_ref, seq=seq, tk=tk)
    lf = lam_ref[...]
    lam = (jnp.exp(jnp.sum(lf[0:1, :] * lf[1:2, :], axis=1, keepdims=True))
           - jnp.exp(jnp.sum(lf[2:3, :] * lf[3:4, :], axis=1, keepdims=True)) + lam_init)
    o = on[0:tq, :] - lam * on[tq:2 * tq, :]
    ms = jnp.mean(o * o, axis=-1, keepdims=True)
    o_ref[...] = ((o * lax.rsqrt(ms + EPS) * sg_ref[...]) * (1.0 - lam_init)).astype(BF16)


def _diff_flash_call(q, k, v, lam, subln_g, layer_idx):
    b, s, d = q.shape
    v_dim = subln_g.shape[0]
    assert v_dim == LANES and d == DIFF_HEADS * v_dim
    tq, tk = min(Q_TILE, s), min(KV_TILE, s)
    lam_init = 0.8 - 0.6 * math.exp(-0.3 * layer_idx)
    kern = functools.partial(_diff_flash_kernel, seq=s, tq=tq, tk=tk, half=v_dim // 2, lam_init=lam_init)
    kv = pl.BlockSpec((None, s, v_dim), lambda bi, h, i: (bi, 0, h))
    qo = pl.BlockSpec((None, tq, v_dim), lambda bi, h, i: (bi, i, h))
    return pl.pallas_call(
        kern,
        grid=(b, DIFF_HEADS, s // tq),
        in_specs=[qo, kv, kv,
                  pl.BlockSpec(lam.shape, lambda bi, h, i: (0, 0)),
                  pl.BlockSpec((1, v_dim), lambda bi, h, i: (0, 0))],
        out_specs=qo,
        out_shape=jax.ShapeDtypeStruct((b, s, d), BF16),
        scratch_shapes=_flash_scratch(2 * tq, tk),
        compiler_params=_params("parallel", "parallel", "parallel"),
        name="diff_flash",
    )(q, k, v, lam, subln_g.reshape(1, v_dim))


def _gqa_flash_kernel(q_ref, k_ref, v_ref, o_ref, qs_ref, p_ref, m_ref, l_ref, acc_ref, *, seq, tq, tk, group):
    for r in range(group):
        qs_ref[r * tq:(r + 1) * tq, :] = q_ref[:, r * LANES:(r + 1) * LANES]
    on = _flash_sweep(qs_ref, k_ref, v_ref, p_ref, m_ref, l_ref, acc_ref, seq=seq, tk=tk)
    for r in range(group):
        o_ref[:, r * LANES:(r + 1) * LANES] = on[r * tq:(r + 1) * tq, :].astype(BF16)


def _gqa_flash_call(q, k, v):
    b, s, d = q.shape
    hd = LANES
    group = GQA_HEADS // GQA_KV_HEADS
    tq, tk = min(GQA_Q_TILE, s), min(KV_TILE, s)
    kern = functools.partial(_gqa_flash_kernel, seq=s, tq=tq, tk=tk, group=group)
    kv = pl.BlockSpec((None, s, hd), lambda bi, h, i: (bi, 0, h))
    qo = pl.BlockSpec((None, tq, group * hd), lambda bi, h, i: (bi, i, h))
    return pl.pallas_call(
        kern,
        grid=(b, GQA_KV_HEADS, s // tq),
        in_specs=[qo, kv, kv],
        out_specs=qo,
        out_shape=jax.ShapeDtypeStruct((b, s, d), BF16),
        scratch_shapes=_flash_scratch(group * tq, tk),
        compiler_params=_params("parallel", "parallel", "parallel"),
        name="gqa_flash",
    )(q, k, v)


def _oproj_kernel(x_ref, a_ref, m_ref, w_ref, o_ref):
    o_ref[...] = x_ref[...] + m_ref[2:3, :] * _dot(a_ref[...], w_ref[...])


def _oproj_call(x, a, m3, w_o):
    b, s, d = x.shape
    tm = min(TOKEN_TILE, s)
    tok = pl.BlockSpec((None, tm, d), lambda bi, i: (bi, i, 0))
    return pl.pallas_call(
        _oproj_kernel,
        grid=(b, s // tm),
        in_specs=[tok, tok,
                  pl.BlockSpec((None, 3, d), lambda bi, i: (bi, 0, 0)),
                  pl.BlockSpec((d, d), lambda bi, i: (0, 0))],
        out_specs=tok,
        out_shape=jax.ShapeDtypeStruct((b, s, d), F32),
        compiler_params=_params("parallel", "parallel"),
        name="attn_oproj",
    )(x, a, m3, w_o)


def _final_norm_kernel(x_ref, g_ref, o_ref):
    x = x_ref[...]
    ms = jnp.mean(x * x, axis=-1, keepdims=True)
    o_ref[...] = x * lax.rsqrt(ms + EPS) * g_ref[...]


def _final_norm_call(x, g):
    b, s, d = x.shape
    tm = min(TOKEN_TILE, s)
    tok = pl.BlockSpec((None, tm, d), lambda bi, i: (bi, i, 0))
    return pl.pallas_call(
        _final_norm_kernel,
        grid=(b, s // tm),
        in_specs=[tok, pl.BlockSpec((1, d), lambda bi, i: (0, 0))],
        out_specs=tok,
        out_shape=jax.ShapeDtypeStruct((b, s, d), F32),
        compiler_params=_params("parallel", "parallel"),
        name="final_norm",
    )(x, g.reshape(1, d))


def kernel(x, c, mod_w, mod_b, norm_g, ffn_w_gu, ffn_w_down, pool_w, pool_scale,
           diff_w_qkv, diff_lambda, diff_subln_g, diff_w_o,
           gqa_w_qkv, gqa_q_norm_g, gqa_k_norm_g, gqa_w_o,
           conv_w_in, conv_w, conv_w_out, final_g):
    b, s, d = x.shape
    depth = mod_w.shape[0]
    diff_head_dim = diff_lambda.shape[1]
    diff_tables = _diff_rope_tables(s, diff_head_dim, diff_head_dim // 4)
    axial_tables = _axial_rope_tables(s, gqa_q_norm_g.shape[0] // 2)

    mods = _mod_call(c, mod_w, mod_b).reshape(depth, b, N_MOD, d)
    w_gu, w_down = ffn_w_gu.astype(BF16), ffn_w_down.astype(BF16)

    for i in range(depth):
        m_ffn0, m_mix, m_ffn1 = (mods[i, :, 3 * k:3 * k + 3, :] for k in range(3))
        x = _ffn_call(x, m_ffn0, norm_g[i, 0], w_gu[i, 0], w_down[i, 0])
        kind = i % 4
        if kind == 0:
            x = _pool_call(x, m_mix, norm_g[i, 1], pool_w.astype(BF16), pool_scale)
        elif kind == 1:
            q, k, v = _diff_qkv_call(x, m_mix, norm_g[i, 1], diff_w_qkv.astype(BF16), diff_tables,
                                     diff_head_dim, diff_head_dim // 4)
            a = _diff_flash_call(q, k, v, diff_lambda, diff_subln_g, i)
            x = _oproj_call(x, a, m_mix, diff_w_o.astype(BF16))
        elif kind == 2:
            q, k, v = _gqa_qkv_call(x, m_mix, norm_g[i, 1], gqa_w_qkv.astype(BF16),
                                    gqa_q_norm_g, gqa_k_norm_g, axial_tables)
            a = _gqa_flash_call(q, k, v)
            x = _oproj_call(x, a, m_mix, gqa_w_o.astype(BF16))
        else:
            x = _conv_call(x, m_mix, norm_g[i, 1], conv_w_in.astype(BF16), conv_w, conv_w_out.astype(BF16))
        x = _ffn_call(x, m_ffn1, norm_g[i, 2], w_gu[i, 1], w_down[i, 1])

    return _final_norm_call(x, final_g)
```

```python
import functools
import math

import jax
import jax.numpy as jnp
from jax import lax
from jax.experimental import pallas as pl
from jax.experimental.pallas import tpu as pltpu

F32 = jnp.float32
BF16 = jnp.bfloat16

LANES = 128
SUBLANES = 8
VMEM_LIMIT_BYTES = 56 * 1024 * 1024

EPS = 1e-6
LOG2_E = math.log2(math.e)
GRID_W = 64
POOL_WINDOWS = (2, 4, 8, 16)
DIFF_HEADS = 8
ROPE_THETA = 500000.0
GQA_HEADS = 8
GQA_KV_HEADS = 2
AXIAL_THETA = 10000.0
N_MOD = 9

TOKEN_TILE = 512
FFN_CHUNK = 256
Q_TILE = 512
GQA_Q_TILE = 256
KV_TILE = 1024
HALO = SUBLANES


def _params(*sem):
    return pltpu.CompilerParams(dimension_semantics=sem, vmem_limit_bytes=VMEM_LIMIT_BYTES)


def _silu(x):
    return x * jax.nn.sigmoid(x)


def _norm_mod(x, g, m_ref):
    ms = jnp.mean(x * x, axis=-1, keepdims=True)
    y = x * lax.rsqrt(ms + EPS) * g
    return y * (1.0 + m_ref[1:2, :]) + m_ref[0:1, :]


def _dot(a, b):
    return jnp.dot(a, b, preferred_element_type=F32)


def _mod_kernel(c_ref, w_ref, b_ref, o_ref):
    ca = _silu(c_ref[...]).astype(BF16)
    o_ref[...] = _dot(ca, w_ref[...].astype(BF16)) + b_ref[...]


def _mod_call(c, mod_w, mod_b):
    depth, d, nd = mod_w.shape
    b = c.shape[0]
    tn = d
    return pl.pallas_call(
        _mod_kernel,
        grid=(depth, nd // tn),
        in_specs=[
            pl.BlockSpec((b, d), lambda l, j: (0, 0)),
            pl.BlockSpec((None, d, tn), lambda l, j: (l, 0, j)),
            pl.BlockSpec((None, 1, tn), lambda l, j: (l, 0, j)),
        ],
        out_specs=pl.BlockSpec((None, b, tn), lambda l, j: (l, 0, j)),
        out_shape=jax.ShapeDtypeStruct((depth, b, nd), F32),
        compiler_params=_params("parallel", "parallel"),
        name="adaln_mod",
    )(c, mod_w, mod_b.reshape(depth, 1, nd))


def _ffn_kernel(x_ref, m_ref, g_ref, wgu_ref, wd_ref, o_ref, act_ref, *, d_ff, chunk):
    x = x_ref[...]
    h = _norm_mod(x, g_ref[...], m_ref).astype(BF16)
    for c0 in range(0, d_ff, chunk):
        gate = _dot(h, wgu_ref[:, c0:c0 + chunk])
        up = _dot(h, wgu_ref[:, d_ff + c0:d_ff + c0 + chunk])
        act_ref[:, c0:c0 + chunk] = (_silu(gate) * up).astype(BF16)
    y = _dot(act_ref[...], wd_ref[...])
    o_ref[...] = x + (0.5 * m_ref[2:3, :]) * y


def _ffn_call(x, m3, g, w_gu, w_down):
    b, s, d = x.shape
    d_ff = w_down.shape[0]
    tm = min(TOKEN_TILE, s)
    chunk = FFN_CHUNK if d_ff % FFN_CHUNK == 0 else d_ff
    kern = functools.partial(_ffn_kernel, d_ff=d_ff, chunk=chunk)
    return pl.pallas_call(
        kern,
        grid=(b, s // tm),
        in_specs=[
            pl.BlockSpec((None, tm, d), lambda bi, i: (bi, i, 0)),
            pl.BlockSpec((None, 3, d), lambda bi, i: (bi, 0, 0)),
            pl.BlockSpec((1, d), lambda bi, i: (0, 0)),
            pl.BlockSpec((d, 2 * d_ff), lambda bi, i: (0, 0)),
            pl.BlockSpec((d_ff, d), lambda bi, i: (0, 0)),
        ],
        out_specs=pl.BlockSpec((None, tm, d), lambda bi, i: (bi, i, 0)),
        out_shape=jax.ShapeDtypeStruct((b, s, d), F32),
        scratch_shapes=[pltpu.VMEM((tm, d_ff), BF16)],
        compiler_params=_params("parallel", "parallel"),
        name="ffn",
    )(x, m3, g.reshape(1, d), w_gu, w_down)


def _halo_specs(tm, d, s):
    per = tm // HALO
    last = s // HALO - 1
    return [
        pl.BlockSpec((None, tm, d), lambda bi, i: (bi, i, 0)),
        pl.BlockSpec((None, HALO, d), lambda bi, i: (bi, jnp.maximum(i * per - 1, 0), 0)),
        pl.BlockSpec((None, HALO, d), lambda bi, i: (bi, jnp.minimum((i + 1) * per, last), 0)),
    ]


def _fill_halo_scratch(hs_ref, x_ref, xp_ref, xn_ref, m_ref, g, tm):
    i = pl.program_id(1)
    n = pl.num_programs(1)
    h = _norm_mod(x_ref[...], g, m_ref)
    hp = _norm_mod(xp_ref[...], g, m_ref)
    hn = _norm_mod(xn_ref[...], g, m_ref)
    hs_ref[0:HALO, :] = jnp.where(i > 0, hp, 0.0)
    hs_ref[HALO:HALO + tm, :] = h
    hs_ref[HALO + tm:HALO + tm + HALO, :] = jnp.where(i < n - 1, hn, 0.0)
    return h


def _pool_kernel(x_ref, xp_ref, xn_ref, m_ref, g_ref, pw_ref, ps_ref, o_ref, hs_ref, *, seq, tm, group):
    i = pl.program_id(1)
    h = _fill_halo_scratch(hs_ref, x_ref, xp_ref, xn_ref, m_ref, g_ref[...], tm)
    t_abs = i * tm + lax.broadcasted_iota(jnp.int32, (tm, 1), 0)
    for gi, win in enumerate(POOL_WINDOWS):
        half = win // 2
        c0, c1 = gi * group, (gi + 1) * group
        acc = hs_ref[HALO - half:HALO - half + tm, c0:c1]
        for k in range(-half + 1, half):
            acc = acc + hs_ref[HALO + k:HALO + k + tm, c0:c1]
        lo = jnp.maximum(t_abs - half, 0)
        hi = jnp.minimum(t_abs + half, seq)
        cnt = (hi - lo).astype(F32)
        diff = acc / cnt - h[:, c0:c1]
        y = _dot(diff.astype(BF16), pw_ref[gi]) * ps_ref[:, c0:c1]
        o_ref[:, c0:c1] = x_ref[:, c0:c1] + m_ref[2:3, c0:c1] * y


def _pool_call(x, m3, g, pool_w, pool_scale):
    b, s, d = x.shape
    n_groups, group, _ = pool_w.shape
    assert max(POOL_WINDOWS) // 2 <= HALO
    tm = min(TOKEN_TILE, s)
    kern = functools.partial(_pool_kernel, seq=s, tm=tm, group=group)
    return pl.pallas_call(
        kern,
        grid=(b, s // tm),
        in_specs=_halo_specs(tm, d, s) + [
            pl.BlockSpec((None, 3, d), lambda bi, i: (bi, 0, 0)),
            pl.BlockSpec((1, d), lambda bi, i: (0, 0)),
            pl.BlockSpec((n_groups, group, group), lambda bi, i: (0, 0, 0)),
            pl.BlockSpec((1, d), lambda bi, i: (0, 0)),
        ],
        out_specs=pl.BlockSpec((None, tm, d), lambda bi, i: (bi, i, 0)),
        out_shape=jax.ShapeDtypeStruct((b, s, d), F32),
        scratch_shapes=[pltpu.VMEM((tm + 2 * HALO, d), F32)],
        compiler_params=_params("parallel", "parallel"),
        name="pool_mixer",
    )(x, x, x, m3, g.reshape(1, d), pool_w, pool_scale.reshape(1, d))


def _conv_kernel(x_ref, xp_ref, xn_ref, m_ref, g_ref, win_ref, cw_ref, wout_ref, o_ref, hs_ref, zs_ref, *, tm, d):
    _fill_halo_scratch(hs_ref, x_ref, xp_ref, xn_ref, m_ref, g_ref[...], tm)
    proj = _dot(hs_ref[...].astype(BF16), win_ref[...])
    gb = proj[HALO:HALO + tm, 0:d]
    zs_ref[...] = proj[:, d:2 * d] * proj[:, 2 * d:3 * d]
    zc = (cw_ref[0:1, :] * zs_ref[HALO - 1:HALO - 1 + tm, :]
          + cw_ref[1:2, :] * zs_ref[HALO:HALO + tm, :]
          + cw_ref[2:3, :] * zs_ref[HALO + 1:HALO + 1 + tm, :])
    y = _dot((gb * zc).astype(BF16), wout_ref[...])
    o_ref[...] = x_ref[...] + m_ref[2:3, :] * y


def _conv_call(x, m3, g, w_in, conv_w, w_out):
    b, s, d = x.shape
    assert conv_w.shape[0] == 3
    tm = min(TOKEN_TILE, s)
    kern = functools.partial(_conv_kernel, tm=tm, d=d)
    return pl.pallas_call(
        kern,
        grid=(b, s // tm),
        in_specs=_halo_specs(tm, d, s) + [
            pl.BlockSpec((None, 3, d), lambda bi, i: (bi, 0, 0)),
            pl.BlockSpec((1, d), lambda bi, i: (0, 0)),
            pl.BlockSpec((d, 3 * d), lambda bi, i: (0, 0)),
            pl.BlockSpec((3, d), lambda bi, i: (0, 0)),
            pl.BlockSpec((d, d), lambda bi, i: (0, 0)),
        ],
        out_specs=pl.BlockSpec((None, tm, d), lambda bi, i: (bi, i, 0)),
        out_shape=jax.ShapeDtypeStruct((b, s, d), F32),
        scratch_shapes=[pltpu.VMEM((tm + 2 * HALO, d), F32), pltpu.VMEM((tm + 2 * HALO, d), F32)],
        compiler_params=_params("parallel", "parallel"),
        name="conv_mixer",
    )(x, x, x, m3, g.reshape(1, d), w_in, conv_w, w_out)


def _rope_angles(pos, dim, theta):
    inv = 1.0 / (theta ** (jnp.arange(0, dim, 2, dtype=F32) / dim))
    ang = pos.astype(F32)[:, None] * inv[None, :]
    return jnp.cos(ang), jnp.sin(ang)


def _diff_rope_tables(s, head_dim, rot_dim):
    cos, sin = _rope_angles(jnp.arange(s), rot_dim, ROPE_THETA)
    half = rot_dim // 2
    ones = jnp.ones((s, head_dim - rot_dim), F32)
    zeros = jnp.zeros((s, head_dim - rot_dim), F32)
    zh = jnp.zeros((s, half), F32)
    c = jnp.concatenate([cos, cos, ones], axis=1)
    s1 = jnp.concatenate([zh, sin, zeros], axis=1)
    s2 = jnp.concatenate([-sin, zh, zeros], axis=1)
    reps = LANES // head_dim
    return tuple(jnp.tile(t, (1, reps)) for t in (c, s1, s2))


def _axial_rope_tables(s, axial_dim):
    rows = s // GRID_W
    row_pos = jnp.broadcast_to(jnp.arange(rows)[:, None], (rows, GRID_W)).reshape(-1)
    col_pos = jnp.broadcast_to(jnp.arange(GRID_W)[None, :], (rows, GRID_W)).reshape(-1)
    cr, sr = _rope_angles(row_pos, axial_dim, AXIAL_THETA)
    cc, sc = _rope_angles(col_pos, axial_dim, AXIAL_THETA)
    z = jnp.zeros_like(sr)
    c = jnp.concatenate([cr, cr, cc, cc], axis=1)
    s1 = jnp.concatenate([z, sr, z, sc], axis=1)
    s2 = jnp.concatenate([-sr, z, -sc, z], axis=1)
    return c, s1, s2


def _rotate(t, shift, c, s1, s2):
    return t * c + pltpu.roll(t, shift, 1) * s1 + pltpu.roll(t, LANES - shift, 1) * s2


def _diff_qkv_kernel(x_ref, m_ref, g_ref, w_ref, c_ref, s1_ref, s2_ref, q_ref, k_ref, v_ref, *, d, shift, scale):
    h = _norm_mod(x_ref[...], g_ref[...], m_ref).astype(BF16)
    proj = _dot(h, w_ref[...])
    c, s1, s2 = c_ref[...], s1_ref[...], s2_ref[...]
    for j in range(d // LANES):
        lo, hi = j * LANES, (j + 1) * LANES
        q_ref[:, lo:hi] = (_rotate(proj[:, lo:hi], shift, c, s1, s2) * scale).astype(BF16)
        k_ref[:, lo:hi] = _rotate(proj[:, d + lo:d + hi], shift, c, s1, s2).astype(BF16)
    v_ref[...] = proj[:, 2 * d:3 * d].T.astype(BF16)


def _diff_qkv_call(x, m3, g, w_qkv, tables, head_dim, rot_dim):
    b, s, d = x.shape
    tm = min(TOKEN_TILE, s)
    kern = functools.partial(_diff_qkv_kernel, d=d, shift=rot_dim // 2, scale=head_dim ** -0.5 * LOG2_E)
    tok = pl.BlockSpec((None, tm, d), lambda bi, i: (bi, i, 0))
    tab = pl.BlockSpec((tm, LANES), lambda bi, i: (i, 0))
    out = jax.ShapeDtypeStruct((b, s, d), BF16)
    return pl.pallas_call(
        kern,
        grid=(b, s // tm),
        in_specs=[
            tok,
            pl.BlockSpec((None, 3, d), lambda bi, i: (bi, 0, 0)),
            pl.BlockSpec((1, d), lambda bi, i: (0, 0)),
            pl.BlockSpec((d, 3 * d), lambda bi, i: (0, 0)),
            tab, tab, tab,
        ],
        out_specs=[tok, tok, pl.BlockSpec((None, d, tm), lambda bi, i: (bi, 0, i))],
        out_shape=[out, out, jax.ShapeDtypeStruct((b, d, s), BF16)],
        compiler_params=_params("parallel", "parallel"),
        name="diff_qkv",
    )(x, m3, g.reshape(1, d), w_qkv, *tables)


def _gqa_qkv_kernel(x_ref, m_ref, g_ref, w_ref, qg_ref, kg_ref, c_ref, s1_ref, s2_ref,
                    q_ref, k_ref, v_ref, *, n_q, n_kv, shift, scale):
    h = _norm_mod(x_ref[...], g_ref[...], m_ref).astype(BF16)
    proj = _dot(h, w_ref[...])
    c, s1, s2 = c_ref[...], s1_ref[...], s2_ref[...]

    def head(j, gain):
        t = proj[:, j * LANES:(j + 1) * LANES]
        ms = jnp.mean(t * t, axis=-1, keepdims=True)
        return _rotate(t * lax.rsqrt(ms + EPS) * gain, shift, c, s1, s2)

    for j in range(n_q):
        q_ref[:, j * LANES:(j + 1) * LANES] = (head(j, qg_ref[...]) * scale).astype(BF16)
    for j in range(n_kv):
        k_ref[:, j * LANES:(j + 1) * LANES] = head(n_q + j, kg_ref[...]).astype(BF16)
    v_ref[...] = proj[:, (n_q + n_kv) * LANES:].T.astype(BF16)


def _gqa_qkv_call(x, m3, g, w_qkv, q_gain, k_gain, tables):
    b, s, d = x.shape
    hd = q_gain.shape[0]
    assert hd == LANES
    n_q, n_kv = GQA_HEADS, GQA_KV_HEADS
    tm = min(TOKEN_TILE, s)
    kern = functools.partial(_gqa_qkv_kernel, n_q=n_q, n_kv=n_kv, shift=hd // 4, scale=hd ** -0.5 * LOG2_E)
    tab = pl.BlockSpec((tm, LANES), lambda bi, i: (i, 0))

    def tok(width):
        return pl.BlockSpec((None, tm, width), lambda bi, i: (bi, i, 0))

    return pl.pallas_call(
        kern,
        grid=(b, s // tm),
        in_specs=[
            tok(d),
            pl.BlockSpec((None, 3, d), lambda bi, i: (bi, 0, 0)),
            pl.BlockSpec((1, d), lambda bi, i: (0, 0)),
            pl.BlockSpec(w_qkv.shape, lambda bi, i: (0, 0)),
            pl.BlockSpec((1, hd), lambda bi, i: (0, 0)),
            pl.BlockSpec((1, hd), lambda bi, i: (0, 0)),
            tab, tab, tab,
        ],
        out_specs=[tok(n_q * hd), tok(n_kv * hd),
                   pl.BlockSpec((None, n_kv * hd, tm), lambda bi, i: (bi, 0, i))],
        out_shape=[jax.ShapeDtypeStruct((b, s, n_q * hd), BF16),
                   jax.ShapeDtypeStruct((b, s, n_kv * hd), BF16),
                   jax.ShapeDtypeStruct((b, n_kv * hd, s), BF16)],
        compiler_params=_params("parallel", "parallel"),
        name="gqa_qkv",
    )(x, m3, g.reshape(1, d), w_qkv, q_gain.reshape(1, hd), k_gain.reshape(1, hd), *tables)


def _flash_sweep(qs_ref, k_ref, vt_ref, scratch, *, seq, tk):
    s_refs, p_refs, m_ref, l_ref, a_ref, acc_ref = scratch[0:2], scratch[2:4], *scratch[4:8]
    n = seq // tk
    assert n % 2 == 0 and n * tk == seq

    def slab(j):
        return pl.ds(pl.multiple_of(j * tk, tk), tk)

    def scores_into(s_ref, j):
        s_ref[...] = lax.dot_general(k_ref[slab(j), :], qs_ref[...], (((1,), (1,)), ((), ())),
                                     preferred_element_type=F32)

    def accumulate(p_ref, j):
        acc_ref[...] = a_ref[...] * acc_ref[...] + _dot(vt_ref[:, slab(j)], p_ref[...])

    def softmax_into(p_ref, s_ref):
        s = s_ref[...]
        m_prev = m_ref[...]
        m_next = jnp.maximum(m_prev, jnp.max(s, axis=0, keepdims=True))
        alpha = jnp.exp2(m_prev - m_next)
        p = jnp.exp2(s - m_next)
        m_ref[...] = m_next
        l_ref[...] = alpha * l_ref[...] + jnp.sum(p, axis=0, keepdims=True)
        p_ref[...] = p.astype(BF16)
        return alpha

    def one_pass(s_cur, s_nxt, p_cur, p_prv, j):
        scores_into(s_nxt, jnp.minimum(j + 1, n - 1))
        accumulate(p_prv, jnp.maximum(j - 1, 0))
        a_ref[...] = softmax_into(p_cur, s_cur)

    m_ref[...] = jnp.full(m_ref.shape, -jnp.inf, F32)
    l_ref[...] = jnp.zeros(l_ref.shape, F32)
    a_ref[...] = jnp.zeros(a_ref.shape, F32)
    acc_ref[...] = jnp.zeros(acc_ref.shape, F32)
    p_refs[1][...] = jnp.zeros(p_refs[1].shape, BF16)
    scores_into(s_refs[0], 0)

    def step(i, carry):
        one_pass(s_refs[0], s_refs[1], p_refs[0], p_refs[1], 2 * i)
        one_pass(s_refs[1], s_refs[0], p_refs[1], p_refs[0], 2 * i + 1)
        return carry

    lax.fori_loop(0, n // 2, step, 0)
    accumulate(p_refs[1], n - 1)
    return acc_ref[...] / l_ref[...]


def _flash_scratch(cols, tk, v_dim):
    return [pltpu.VMEM((cols, LANES), BF16),
            pltpu.VMEM((tk, cols), F32), pltpu.VMEM((tk, cols), F32),
            pltpu.VMEM((tk, cols), BF16), pltpu.VMEM((tk, cols), BF16),
            pltpu.VMEM((1, cols), F32), pltpu.VMEM((1, cols), F32), pltpu.VMEM((1, cols), F32),
            pltpu.VMEM((v_dim, cols), F32)]


def _kv_tile(s):
    tk = min(KV_TILE, s // 2)
    assert s % (2 * tk) == 0
    return tk


def _diff_flash_kernel(q_ref, k_ref, vt_ref, lam_ref, sg_ref, o_ref, qs_ref, *scratch,
                       seq, tq, tk, half, lam_init):
    q = q_ref[...]
    lane = lax.broadcasted_iota(jnp.int32, q.shape, 1)
    zero = jnp.zeros_like(q)
    qs_ref[0:tq, :] = jnp.where(lane < half, q, zero)
    qs_ref[tq:2 * tq, :] = jnp.where(lane >= half, q, zero)
    ont = _flash_sweep(qs_ref, k_ref, vt_ref, scratch, seq=seq, tk=tk)
    lf = lam_ref[...]
    lam = (jnp.exp(jnp.sum(lf[0:1, :] * lf[1:2, :], axis=1, keepdims=True))
           - jnp.exp(jnp.sum(lf[2:3, :] * lf[3:4, :], axis=1, keepdims=True)) + lam_init)
    o = (ont[:, 0:tq] - lam * ont[:, tq:2 * tq]).T
    ms = jnp.mean(o * o, axis=-1, keepdims=True)
    o_ref[...] = ((o * lax.rsqrt(ms + EPS) * sg_ref[...]) * (1.0 - lam_init)).astype(BF16)


def _diff_flash_call(q, k, vt, lam, subln_g, layer_idx):
    b, s, d = q.shape
    v_dim = subln_g.shape[0]
    assert v_dim == LANES and d == DIFF_HEADS * v_dim
    tq, tk = min(Q_TILE, s), _kv_tile(s)
    lam_init = 0.8 - 0.6 * math.exp(-0.3 * layer_idx)
    kern = functools.partial(_diff_flash_kernel, seq=s, tq=tq, tk=tk, half=v_dim // 2, lam_init=lam_init)
    qo = pl.BlockSpec((None, tq, v_dim), lambda bi, h, i: (bi, i, h))
    return pl.pallas_call(
        kern,
        grid=(b, DIFF_HEADS, s // tq),
        in_specs=[qo,
                  pl.BlockSpec((None, s, v_dim), lambda bi, h, i: (bi, 0, h)),
                  pl.BlockSpec((None, v_dim, s), lambda bi, h, i: (bi, h, 0)),
                  pl.BlockSpec(lam.shape, lambda bi, h, i: (0, 0)),
                  pl.BlockSpec((1, v_dim), lambda bi, h, i: (0, 0))],
        out_specs=qo,
        out_shape=jax.ShapeDtypeStruct((b, s, d), BF16),
        scratch_shapes=_flash_scratch(2 * tq, tk, v_dim),
        compiler_params=_params("parallel", "parallel", "parallel"),
        name="diff_flash",
    )(q, k, vt, lam, subln_g.reshape(1, v_dim))


def _gqa_flash_kernel(q_ref, k_ref, vt_ref, o_ref, qs_ref, *scratch, seq, tq, tk, group):
    for r in range(group):
        qs_ref[r * tq:(r + 1) * tq, :] = q_ref[:, r * LANES:(r + 1) * LANES]
    ont = _flash_sweep(qs_ref, k_ref, vt_ref, scratch, seq=seq, tk=tk)
    for r in range(group):
        o_ref[:, r * LANES:(r + 1) * LANES] = ont[:, r * tq:(r + 1) * tq].T.astype(BF16)


def _gqa_flash_call(q, k, vt):
    b, s, d = q.shape
    hd = LANES
    group = GQA_HEADS // GQA_KV_HEADS
    tq, tk = min(GQA_Q_TILE, s), _kv_tile(s)
    kern = functools.partial(_gqa_flash_kernel, seq=s, tq=tq, tk=tk, group=group)
    qo = pl.BlockSpec((None, tq, group * hd), lambda bi, h, i: (bi, i, h))
    return pl.pallas_call(
        kern,
        grid=(b, GQA_KV_HEADS, s // tq),
        in_specs=[qo,
                  pl.BlockSpec((None, s, hd), lambda bi, h, i: (bi, 0, h)),
                  pl.BlockSpec((None, hd, s), lambda bi, h, i: (bi, h, 0))],
        out_specs=qo,
        out_shape=jax.ShapeDtypeStruct((b, s, d), BF16),
        scratch_shapes=_flash_scratch(group * tq, tk, hd),
        compiler_params=_params("parallel", "parallel", "parallel"),
        name="gqa_flash",
    )(q, k, vt)


def _oproj_kernel(x_ref, a_ref, m_ref, w_ref, o_ref):
    o_ref[...] = x_ref[...] + m_ref[2:3, :] * _dot(a_ref[...], w_ref[...])


def _oproj_call(x, a, m3, w_o):
    b, s, d = x.shape
    tm = min(TOKEN_TILE, s)
    tok = pl.BlockSpec((None, tm, d), lambda bi, i: (bi, i, 0))
    return pl.pallas_call(
        _oproj_kernel,
        grid=(b, s // tm),
        in_specs=[tok, tok,
                  pl.BlockSpec((None, 3, d), lambda bi, i: (bi, 0, 0)),
                  pl.BlockSpec((d, d), lambda bi, i: (0, 0))],
        out_specs=tok,
        out_shape=jax.ShapeDtypeStruct((b, s, d), F32),
        compiler_params=_params("parallel", "parallel"),
        name="attn_oproj",
    )(x, a, m3, w_o)


def _final_norm_kernel(x_ref, g_ref, o_ref):
    x = x_ref[...]
    ms = jnp.mean(x * x, axis=-1, keepdims=True)
    o_ref[...] = x * lax.rsqrt(ms + EPS) * g_ref[...]


def _final_norm_call(x, g):
    b, s, d = x.shape
    tm = min(TOKEN_TILE, s)
    tok = pl.BlockSpec((None, tm, d), lambda bi, i: (bi, i, 0))
    return pl.pallas_call(
        _final_norm_kernel,
        grid=(b, s // tm),
        in_specs=[tok, pl.BlockSpec((1, d), lambda bi, i: (0, 0))],
        out_specs=tok,
        out_shape=jax.ShapeDtypeStruct((b, s, d), F32),
        compiler_params=_params("parallel", "parallel"),
        name="final_norm",
    )(x, g.reshape(1, d))


def kernel(x, c, mod_w, mod_b, norm_g, ffn_w_gu, ffn_w_down, pool_w, pool_scale,
           diff_w_qkv, diff_lambda, diff_subln_g, diff_w_o,
           gqa_w_qkv, gqa_q_norm_g, gqa_k_norm_g, gqa_w_o,
           conv_w_in, conv_w, conv_w_out, final_g):
    b, s, d = x.shape
    depth = mod_w.shape[0]
    diff_head_dim = diff_lambda.shape[1]
    diff_tables = _diff_rope_tables(s, diff_head_dim, diff_head_dim // 4)
    axial_tables = _axial_rope_tables(s, gqa_q_norm_g.shape[0] // 2)

    mods = _mod_call(c, mod_w, mod_b).reshape(depth, b, N_MOD, d)
    w_gu, w_down = ffn_w_gu.astype(BF16), ffn_w_down.astype(BF16)

    for i in range(depth):
        m_ffn0, m_mix, m_ffn1 = (mods[i, :, 3 * k:3 * k + 3, :] for k in range(3))
        x = _ffn_call(x, m_ffn0, norm_g[i, 0], w_gu[i, 0], w_down[i, 0])
        kind = i % 4
        if kind == 0:
            x = _pool_call(x, m_mix, norm_g[i, 1], pool_w.astype(BF16), pool_scale)
        elif kind == 1:
            q, k, v = _diff_qkv_call(x, m_mix, norm_g[i, 1], diff_w_qkv.astype(BF16), diff_tables,
                                     diff_head_dim, diff_head_dim // 4)
            a = _diff_flash_call(q, k, v, diff_lambda, diff_subln_g, i)
            x = _oproj_call(x, a, m_mix, diff_w_o.astype(BF16))
        elif kind == 2:
            q, k, v = _gqa_qkv_call(x, m_mix, norm_g[i, 1], gqa_w_qkv.astype(BF16),
                                    gqa_q_norm_g, gqa_k_norm_g, axial_tables)
            a = _gqa_flash_call(q, k, v)
            x = _oproj_call(x, a, m_mix, gqa_w_o.astype(BF16))
        else:
            x = _conv_call(x, m_mix, norm_g[i, 1], conv_w_in.astype(BF16), conv_w, conv_w_out.astype(BF16))
        x = _ffn_call(x, m_ffn1, norm_g[i, 2], w_gu[i, 1], w_down[i, 1])

    return _final_norm_call(x, final_g)
```

```python
import functools
import math

import jax
import jax.numpy as jnp
from jax import lax
from jax.experimental import pallas as pl
from jax.experimental.pallas import tpu as pltpu

F32 = jnp.float32
BF16 = jnp.bfloat16

LANES = 128
SUBLANES = 8
VMEM_LIMIT_BYTES = 56 * 1024 * 1024

EPS = 1e-6
LOG2_E = math.log2(math.e)
GRID_W = 64
POOL_WINDOWS = (2, 4, 8, 16)
DIFF_HEADS = 8
ROPE_THETA = 500000.0
GQA_HEADS = 8
GQA_KV_HEADS = 2
AXIAL_THETA = 10000.0
N_MOD = 9

TOKEN_TILE = 512
FFN_CHUNK = 256
Q_TILE = 512
GQA_Q_TILE = 256
KV_TILE = 1024
HALO = SUBLANES


def _params(*sem):
    return pltpu.CompilerParams(dimension_semantics=sem, vmem_limit_bytes=VMEM_LIMIT_BYTES)


def _silu(x):
    return x * jax.nn.sigmoid(x)


def _norm_mod(x, g, m_ref):
    ms = jnp.mean(x * x, axis=-1, keepdims=True)
    y = x * lax.rsqrt(ms + EPS) * g
    return y * (1.0 + m_ref[1:2, :]) + m_ref[0:1, :]


def _dot(a, b):
    return jnp.dot(a, b, preferred_element_type=F32)


def _mod_kernel(c_ref, w_ref, b_ref, o_ref):
    ca = _silu(c_ref[...]).astype(BF16)
    o_ref[...] = _dot(ca, w_ref[...].astype(BF16)) + b_ref[...]


def _mod_call(c, mod_w, mod_b):
    depth, d, nd = mod_w.shape
    b = c.shape[0]
    tn = d
    return pl.pallas_call(
        _mod_kernel,
        grid=(depth, nd // tn),
        in_specs=[
            pl.BlockSpec((b, d), lambda l, j: (0, 0)),
            pl.BlockSpec((None, d, tn), lambda l, j: (l, 0, j)),
            pl.BlockSpec((None, 1, tn), lambda l, j: (l, 0, j)),
        ],
        out_specs=pl.BlockSpec((None, b, tn), lambda l, j: (l, 0, j)),
        out_shape=jax.ShapeDtypeStruct((depth, b, nd), F32),
        compiler_params=_params("parallel", "parallel"),
        name="adaln_mod",
    )(c, mod_w, mod_b.reshape(depth, 1, nd))


def _ffn_body(x, m_ref, g_ref, wgu_ref, wd_ref, fg_ref, o_ref, act_ref, *, d_ff, chunk, final):
    h = _norm_mod(x, g_ref[...], m_ref).astype(BF16)
    for c0 in range(0, d_ff, chunk):
        gate = _dot(h, wgu_ref[:, c0:c0 + chunk])
        up = _dot(h, wgu_ref[:, d_ff + c0:d_ff + c0 + chunk])
        act_ref[:, c0:c0 + chunk] = (_silu(gate) * up).astype(BF16)
    y = _dot(act_ref[...], wd_ref[...])
    out = x + (0.5 * m_ref[2:3, :]) * y
    if final:
        ms = jnp.mean(out * out, axis=-1, keepdims=True)
        out = out * lax.rsqrt(ms + EPS) * fg_ref[...]
    o_ref[...] = out


def _ffn_kernel(x_ref, m_ref, g_ref, wgu_ref, wd_ref, fg_ref, o_ref, act_ref, **kw):
    _ffn_body(x_ref[...], m_ref, g_ref, wgu_ref, wd_ref, fg_ref, o_ref, act_ref, **kw)


def _attn_ffn_kernel(x_ref, a_ref, mm_ref, wo_ref, m_ref, g_ref, wgu_ref, wd_ref, fg_ref, o_ref, act_ref, **kw):
    x = x_ref[...] + mm_ref[2:3, :] * _dot(a_ref[...], wo_ref[...])
    _ffn_body(x, m_ref, g_ref, wgu_ref, wd_ref, fg_ref, o_ref, act_ref, **kw)


def _ffn_call(x, m3, g, w_gu, w_down, layer, which, final_g, *, final=False, attn=None):
    b, s, d = x.shape
    d_ff = w_down.shape[2]
    tm = min(TOKEN_TILE, s)
    chunk = FFN_CHUNK if d_ff % FFN_CHUNK == 0 else d_ff
    tok = pl.BlockSpec((None, tm, d), lambda bi, i: (bi, i, 0))
    mod = pl.BlockSpec((None, 3, d), lambda bi, i: (bi, 0, 0))
    row = pl.BlockSpec((1, d), lambda bi, i: (0, 0))
    in_specs = [mod, row,
                pl.BlockSpec((None, None, d, 2 * d_ff), lambda bi, i: (layer, which, 0, 0)),
                pl.BlockSpec((None, None, d_ff, d), lambda bi, i: (layer, which, 0, 0)),
                row]
    args = [m3, g.reshape(1, d), w_gu, w_down, final_g.reshape(1, d)]
    if attn is None:
        body, in_specs, args = _ffn_kernel, [tok] + in_specs, [x] + args
    else:
        a, m_mix, w_o = attn
        body = _attn_ffn_kernel
        in_specs = [tok, tok, mod, pl.BlockSpec((d, d), lambda bi, i: (0, 0))] + in_specs
        args = [x, a, m_mix, w_o] + args
    return pl.pallas_call(
        functools.partial(body, d_ff=d_ff, chunk=chunk, final=final),
        grid=(b, s // tm),
        in_specs=in_specs,
        out_specs=tok,
        out_shape=jax.ShapeDtypeStruct((b, s, d), F32),
        scratch_shapes=[pltpu.VMEM((tm, d_ff), BF16)],
        compiler_params=_params("parallel", "parallel"),
        name="ffn",
    )(*args)


def _halo_specs(tm, d, s):
    per = tm // HALO
    last = s // HALO - 1
    return [
        pl.BlockSpec((None, tm, d), lambda bi, i: (bi, i, 0)),
        pl.BlockSpec((None, HALO, d), lambda bi, i: (bi, jnp.maximum(i * per - 1, 0), 0)),
        pl.BlockSpec((None, HALO, d), lambda bi, i: (bi, jnp.minimum((i + 1) * per, last), 0)),
    ]


def _fill_halo_scratch(hs_ref, x_ref, xp_ref, xn_ref, m_ref, g, tm):
    i = pl.program_id(1)
    n = pl.num_programs(1)
    h = _norm_mod(x_ref[...], g, m_ref)
    hp = _norm_mod(xp_ref[...], g, m_ref)
    hn = _norm_mod(xn_ref[...], g, m_ref)
    hs_ref[0:HALO, :] = jnp.where(i > 0, hp, 0.0)
    hs_ref[HALO:HALO + tm, :] = h
    hs_ref[HALO + tm:HALO + tm + HALO, :] = jnp.where(i < n - 1, hn, 0.0)
    return h


def _pool_kernel(x_ref, xp_ref, xn_ref, m_ref, g_ref, pw_ref, ps_ref, o_ref, hs_ref, *, seq, tm, group):
    i = pl.program_id(1)
    h = _fill_halo_scratch(hs_ref, x_ref, xp_ref, xn_ref, m_ref, g_ref[...], tm)
    t_abs = i * tm + lax.broadcasted_iota(jnp.int32, (tm, 1), 0)
    for gi, win in enumerate(POOL_WINDOWS):
        half = win // 2
        c0, c1 = gi * group, (gi + 1) * group
        acc = hs_ref[HALO - half:HALO - half + tm, c0:c1]
        for k in range(-half + 1, half):
            acc = acc + hs_ref[HALO + k:HALO + k + tm, c0:c1]
        lo = jnp.maximum(t_abs - half, 0)
        hi = jnp.minimum(t_abs + half, seq)
        cnt = (hi - lo).astype(F32)
        diff = acc / cnt - h[:, c0:c1]
        y = _dot(diff.astype(BF16), pw_ref[gi]) * ps_ref[:, c0:c1]
        o_ref[:, c0:c1] = x_ref[:, c0:c1] + m_ref[2:3, c0:c1] * y


def _pool_call(x, m3, g, pool_w, pool_scale):
    b, s, d = x.shape
    n_groups, group, _ = pool_w.shape
    assert max(POOL_WINDOWS) // 2 <= HALO
    tm = min(TOKEN_TILE, s)
    kern = functools.partial(_pool_kernel, seq=s, tm=tm, group=group)
    return pl.pallas_call(
        kern,
        grid=(b, s // tm),
        in_specs=_halo_specs(tm, d, s) + [
            pl.BlockSpec((None, 3, d), lambda bi, i: (bi, 0, 0)),
            pl.BlockSpec((1, d), lambda bi, i: (0, 0)),
            pl.BlockSpec((n_groups, group, group), lambda bi, i: (0, 0, 0)),
            pl.BlockSpec((1, d), lambda bi, i: (0, 0)),
        ],
        out_specs=pl.BlockSpec((None, tm, d), lambda bi, i: (bi, i, 0)),
        out_shape=jax.ShapeDtypeStruct((b, s, d), F32),
        scratch_shapes=[pltpu.VMEM((tm + 2 * HALO, d), F32)],
        compiler_params=_params("parallel", "parallel"),
        name="pool_mixer",
    )(x, x, x, m3, g.reshape(1, d), pool_w, pool_scale.reshape(1, d))


def _conv_kernel(x_ref, xp_ref, xn_ref, m_ref, g_ref, win_ref, cw_ref, wout_ref, o_ref, hs_ref, zs_ref, *, tm, d):
    _fill_halo_scratch(hs_ref, x_ref, xp_ref, xn_ref, m_ref, g_ref[...], tm)
    proj = _dot(hs_ref[...].astype(BF16), win_ref[...])
    gb = proj[HALO:HALO + tm, 0:d]
    zs_ref[...] = proj[:, d:2 * d] * proj[:, 2 * d:3 * d]
    zc = (cw_ref[0:1, :] * zs_ref[HALO - 1:HALO - 1 + tm, :]
          + cw_ref[1:2, :] * zs_ref[HALO:HALO + tm, :]
          + cw_ref[2:3, :] * zs_ref[HALO + 1:HALO + 1 + tm, :])
    y = _dot((gb * zc).astype(BF16), wout_ref[...])
    o_ref[...] = x_ref[...] + m_ref[2:3, :] * y


def _conv_call(x, m3, g, w_in, conv_w, w_out):
    b, s, d = x.shape
    assert conv_w.shape[0] == 3
    tm = min(TOKEN_TILE, s)
    kern = functools.partial(_conv_kernel, tm=tm, d=d)
    return pl.pallas_call(
        kern,
        grid=(b, s // tm),
        in_specs=_halo_specs(tm, d, s) + [
            pl.BlockSpec((None, 3, d), lambda bi, i: (bi, 0, 0)),
            pl.BlockSpec((1, d), lambda bi, i: (0, 0)),
            pl.BlockSpec((d, 3 * d), lambda bi, i: (0, 0)),
            pl.BlockSpec((3, d), lambda bi, i: (0, 0)),
            pl.BlockSpec((d, d), lambda bi, i: (0, 0)),
        ],
        out_specs=pl.BlockSpec((None, tm, d), lambda bi, i: (bi, i, 0)),
        out_shape=jax.ShapeDtypeStruct((b, s, d), F32),
        scratch_shapes=[pltpu.VMEM((tm + 2 * HALO, d), F32), pltpu.VMEM((tm + 2 * HALO, d), F32)],
        compiler_params=_params("parallel", "parallel"),
        name="conv_mixer",
    )(x, x, x, m3, g.reshape(1, d), w_in, conv_w, w_out)


def _rope_angles(pos, dim, theta):
    inv = 1.0 / (theta ** (jnp.arange(0, dim, 2, dtype=F32) / dim))
    ang = pos.astype(F32)[:, None] * inv[None, :]
    return jnp.cos(ang), jnp.sin(ang)


def _diff_rope_tables(s, head_dim, rot_dim):
    cos, sin = _rope_angles(jnp.arange(s), rot_dim, ROPE_THETA)
    half = rot_dim // 2
    ones = jnp.ones((s, head_dim - rot_dim), F32)
    zeros = jnp.zeros((s, head_dim - rot_dim), F32)
    zh = jnp.zeros((s, half), F32)
    c = jnp.concatenate([cos, cos, ones], axis=1)
    s1 = jnp.concatenate([zh, sin, zeros], axis=1)
    s2 = jnp.concatenate([-sin, zh, zeros], axis=1)
    reps = LANES // head_dim
    return tuple(jnp.tile(t, (1, reps)) for t in (c, s1, s2))


def _axial_rope_tables(s, axial_dim):
    rows = s // GRID_W
    row_pos = jnp.broadcast_to(jnp.arange(rows)[:, None], (rows, GRID_W)).reshape(-1)
    col_pos = jnp.broadcast_to(jnp.arange(GRID_W)[None, :], (rows, GRID_W)).reshape(-1)
    cr, sr = _rope_angles(row_pos, axial_dim, AXIAL_THETA)
    cc, sc = _rope_angles(col_pos, axial_dim, AXIAL_THETA)
    z = jnp.zeros_like(sr)
    c = jnp.concatenate([cr, cr, cc, cc], axis=1)
    s1 = jnp.concatenate([z, sr, z, sc], axis=1)
    s2 = jnp.concatenate([-sr, z, -sc, z], axis=1)
    return c, s1, s2


def _rotate(t, shift, c, s1, s2):
    return t * c + pltpu.roll(t, shift, 1) * s1 + pltpu.roll(t, LANES - shift, 1) * s2


def _diff_qkv_kernel(x_ref, m_ref, g_ref, w_ref, c_ref, s1_ref, s2_ref, q_ref, k_ref, v_ref, *, d, shift, scale):
    h = _norm_mod(x_ref[...], g_ref[...], m_ref).astype(BF16)
    proj = _dot(h, w_ref[...])
    c, s1, s2 = c_ref[...], s1_ref[...], s2_ref[...]
    for j in range(d // LANES):
        lo, hi = j * LANES, (j + 1) * LANES
        q_ref[:, lo:hi] = (_rotate(proj[:, lo:hi], shift, c, s1, s2) * scale).astype(BF16)
        k_ref[:, lo:hi] = _rotate(proj[:, d + lo:d + hi], shift, c, s1, s2).astype(BF16)
    v_ref[...] = proj[:, 2 * d:3 * d].T.astype(BF16)


def _diff_qkv_call(x, m3, g, w_qkv, tables, head_dim, rot_dim):
    b, s, d = x.shape
    tm = min(TOKEN_TILE, s)
    kern = functools.partial(_diff_qkv_kernel, d=d, shift=rot_dim // 2, scale=head_dim ** -0.5 * LOG2_E)
    tok = pl.BlockSpec((None, tm, d), lambda bi, i: (bi, i, 0))
    tab = pl.BlockSpec((tm, LANES), lambda bi, i: (i, 0))
    out = jax.ShapeDtypeStruct((b, s, d), BF16)
    return pl.pallas_call(
        kern,
        grid=(b, s // tm),
        in_specs=[
            tok,
            pl.BlockSpec((None, 3, d), lambda bi, i: (bi, 0, 0)),
            pl.BlockSpec((1, d), lambda bi, i: (0, 0)),
            pl.BlockSpec((d, 3 * d), lambda bi, i: (0, 0)),
            tab, tab, tab,
        ],
        out_specs=[tok, tok, pl.BlockSpec((None, d, tm), lambda bi, i: (bi, 0, i))],
        out_shape=[out, out, jax.ShapeDtypeStruct((b, d, s), BF16)],
        compiler_params=_params("parallel", "parallel"),
        name="diff_qkv",
    )(x, m3, g.reshape(1, d), w_qkv, *tables)


def _gqa_qkv_kernel(x_ref, m_ref, g_ref, w_ref, qg_ref, kg_ref, c_ref, s1_ref, s2_ref,
                    q_ref, k_ref, v_ref, *, n_q, n_kv, shift, scale):
    h = _norm_mod(x_ref[...], g_ref[...], m_ref).astype(BF16)
    proj = _dot(h, w_ref[...])
    c, s1, s2 = c_ref[...], s1_ref[...], s2_ref[...]

    def head(j, gain):
        t = proj[:, j * LANES:(j + 1) * LANES]
        ms = jnp.mean(t * t, axis=-1, keepdims=True)
        return _rotate(t * lax.rsqrt(ms + EPS) * gain, shift, c, s1, s2)

    for j in range(n_q):
        q_ref[:, j * LANES:(j + 1) * LANES] = (head(j, qg_ref[...]) * scale).astype(BF16)
    for j in range(n_kv):
        k_ref[:, j * LANES:(j + 1) * LANES] = head(n_q + j, kg_ref[...]).astype(BF16)
    v_ref[...] = proj[:, (n_q + n_kv) * LANES:].T.astype(BF16)


def _gqa_qkv_call(x, m3, g, w_qkv, q_gain, k_gain, tables):
    b, s, d = x.shape
    hd = q_gain.shape[0]
    assert hd == LANES
    n_q, n_kv = GQA_HEADS, GQA_KV_HEADS
    tm = min(TOKEN_TILE, s)
    kern = functools.partial(_gqa_qkv_kernel, n_q=n_q, n_kv=n_kv, shift=hd // 4, scale=hd ** -0.5 * LOG2_E)
    tab = pl.BlockSpec((tm, LANES), lambda bi, i: (i, 0))

    def tok(width):
        return pl.BlockSpec((None, tm, width), lambda bi, i: (bi, i, 0))

    return pl.pallas_call(
        kern,
        grid=(b, s // tm),
        in_specs=[
            tok(d),
            pl.BlockSpec((None, 3, d), lambda bi, i: (bi, 0, 0)),
            pl.BlockSpec((1, d), lambda bi, i: (0, 0)),
            pl.BlockSpec(w_qkv.shape, lambda bi, i: (0, 0)),
            pl.BlockSpec((1, hd), lambda bi, i: (0, 0)),
            pl.BlockSpec((1, hd), lambda bi, i: (0, 0)),
            tab, tab, tab,
        ],
        out_specs=[tok(n_q * hd), tok(n_kv * hd),
                   pl.BlockSpec((None, n_kv * hd, tm), lambda bi, i: (bi, 0, i))],
        out_shape=[jax.ShapeDtypeStruct((b, s, n_q * hd), BF16),
                   jax.ShapeDtypeStruct((b, s, n_kv * hd), BF16),
                   jax.ShapeDtypeStruct((b, n_kv * hd, s), BF16)],
        compiler_params=_params("parallel", "parallel"),
        name="gqa_qkv",
    )(x, m3, g.reshape(1, d), w_qkv, q_gain.reshape(1, hd), k_gain.reshape(1, hd), *tables)


def _flash_sweep(qs_ref, k_ref, vt_ref, scratch, *, seq, tk):
    s_refs, p_refs, m_ref, l_ref, a_ref, acc_ref = scratch[0:2], scratch[2:4], *scratch[4:8]
    n = seq // tk
    assert n % 2 == 0 and n * tk == seq

    def slab(j):
        return pl.ds(pl.multiple_of(j * tk, tk), tk)

    def scores_into(s_ref, j):
        s_ref[...] = lax.dot_general(k_ref[slab(j), :], qs_ref[...], (((1,), (1,)), ((), ())),
                                     preferred_element_type=F32)

    def accumulate(p_ref, j):
        acc_ref[...] = a_ref[...] * acc_ref[...] + _dot(vt_ref[:, slab(j)], p_ref[...])

    def softmax_into(p_ref, s_ref):
        s = s_ref[...]
        m_prev = m_ref[...]
        m_next = jnp.maximum(m_prev, jnp.max(s, axis=0, keepdims=True))
        alpha = jnp.exp2(m_prev - m_next)
        p = jnp.exp2(s - m_next)
        m_ref[...] = m_next
        l_ref[...] = alpha * l_ref[...] + jnp.sum(p, axis=0, keepdims=True)
        p_ref[...] = p.astype(BF16)
        return alpha

    def one_pass(s_cur, s_nxt, p_cur, p_prv, j):
        scores_into(s_nxt, jnp.minimum(j + 1, n - 1))
        accumulate(p_prv, jnp.maximum(j - 1, 0))
        a_ref[...] = softmax_into(p_cur, s_cur)

    m_ref[...] = jnp.full(m_ref.shape, -jnp.inf, F32)
    l_ref[...] = jnp.zeros(l_ref.shape, F32)
    a_ref[...] = jnp.zeros(a_ref.shape, F32)
    acc_ref[...] = jnp.zeros(acc_ref.shape, F32)
    p_refs[1][...] = jnp.zeros(p_refs[1].shape, BF16)
    scores_into(s_refs[0], 0)

    def step(i, carry):
        one_pass(s_refs[0], s_refs[1], p_refs[0], p_refs[1], 2 * i)
        one_pass(s_refs[1], s_refs[0], p_refs[1], p_refs[0], 2 * i + 1)
        return carry

    lax.fori_loop(0, n // 2, step, 0)
    accumulate(p_refs[1], n - 1)
    return acc_ref[...] / l_ref[...]


def _flash_scratch(cols, tk, v_dim):
    return [pltpu.VMEM((cols, LANES), BF16),
            pltpu.VMEM((tk, cols), F32), pltpu.VMEM((tk, cols), F32),
            pltpu.VMEM((tk, cols), BF16), pltpu.VMEM((tk, cols), BF16),
            pltpu.VMEM((1, cols), F32), pltpu.VMEM((1, cols), F32), pltpu.VMEM((1, cols), F32),
            pltpu.VMEM((v_dim, cols), F32)]


def _kv_tile(s):
    tk = min(KV_TILE, s // 2)
    assert s % (2 * tk) == 0
    return tk


def _diff_flash_kernel(q_ref, k_ref, vt_ref, lam_ref, sg_ref, o_ref, qs_ref, *scratch,
                       seq, tq, tk, half, lam_init):
    q = q_ref[...]
    lane = lax.broadcasted_iota(jnp.int32, q.shape, 1)
    zero = jnp.zeros_like(q)
    qs_ref[0:tq, :] = jnp.where(lane < half, q, zero)
    qs_ref[tq:2 * tq, :] = jnp.where(lane >= half, q, zero)
    ont = _flash_sweep(qs_ref, k_ref, vt_ref, scratch, seq=seq, tk=tk)
    lf = lam_ref[...]
    lam = (jnp.exp(jnp.sum(lf[0:1, :] * lf[1:2, :], axis=1, keepdims=True))
           - jnp.exp(jnp.sum(lf[2:3, :] * lf[3:4, :], axis=1, keepdims=True)) + lam_init)
    o = (ont[:, 0:tq] - lam * ont[:, tq:2 * tq]).T
    ms = jnp.mean(o * o, axis=-1, keepdims=True)
    o_ref[...] = ((o * lax.rsqrt(ms + EPS) * sg_ref[...]) * (1.0 - lam_init)).astype(BF16)


def _diff_flash_call(q, k, vt, lam, subln_g, layer_idx):
    b, s, d = q.shape
    v_dim = subln_g.shape[0]
    assert v_dim == LANES and d == DIFF_HEADS * v_dim
    tq, tk = min(Q_TILE, s), _kv_tile(s)
    lam_init = 0.8 - 0.6 * math.exp(-0.3 * layer_idx)
    kern = functools.partial(_diff_flash_kernel, seq=s, tq=tq, tk=tk, half=v_dim // 2, lam_init=lam_init)
    qo = pl.BlockSpec((None, tq, v_dim), lambda bi, h, i: (bi, i, h))
    return pl.pallas_call(
        kern,
        grid=(b, DIFF_HEADS, s // tq),
        in_specs=[qo,
                  pl.BlockSpec((None, s, v_dim), lambda bi, h, i: (bi, 0, h)),
                  pl.BlockSpec((None, v_dim, s), lambda bi, h, i: (bi, h, 0)),
                  pl.BlockSpec(lam.shape, lambda bi, h, i: (0, 0)),
                  pl.BlockSpec((1, v_dim), lambda bi, h, i: (0, 0))],
        out_specs=qo,
        out_shape=jax.ShapeDtypeStruct((b, s, d), BF16),
        scratch_shapes=_flash_scratch(2 * tq, tk, v_dim),
        compiler_params=_params("parallel", "parallel", "parallel"),
        name="diff_flash",
    )(q, k, vt, lam, subln_g.reshape(1, v_dim))


def _gqa_flash_kernel(q_ref, k_ref, vt_ref, o_ref, qs_ref, *scratch, seq, tq, tk, group):
    for r in range(group):
        qs_ref[r * tq:(r + 1) * tq, :] = q_ref[:, r * LANES:(r + 1) * LANES]
    ont = _flash_sweep(qs_ref, k_ref, vt_ref, scratch, seq=seq, tk=tk)
    for r in range(group):
        o_ref[:, r * LANES:(r + 1) * LANES] = ont[:, r * tq:(r + 1) * tq].T.astype(BF16)


def _gqa_flash_call(q, k, vt):
    b, s, d = q.shape
    hd = LANES
    group = GQA_HEADS // GQA_KV_HEADS
    tq, tk = min(GQA_Q_TILE, s), _kv_tile(s)
    kern = functools.partial(_gqa_flash_kernel, seq=s, tq=tq, tk=tk, group=group)
    qo = pl.BlockSpec((None, tq, group * hd), lambda bi, h, i: (bi, i, h))
    return pl.pallas_call(
        kern,
        grid=(b, GQA_KV_HEADS, s // tq),
        in_specs=[qo,
                  pl.BlockSpec((None, s, hd), lambda bi, h, i: (bi, 0, h)),
                  pl.BlockSpec((None, hd, s), lambda bi, h, i: (bi, h, 0))],
        out_specs=qo,
        out_shape=jax.ShapeDtypeStruct((b, s, d), BF16),
        scratch_shapes=_flash_scratch(group * tq, tk, hd),
        compiler_params=_params("parallel", "parallel", "parallel"),
        name="gqa_flash",
    )(q, k, vt)


def kernel(x, c, mod_w, mod_b, norm_g, ffn_w_gu, ffn_w_down, pool_w, pool_scale,
           diff_w_qkv, diff_lambda, diff_subln_g, diff_w_o,
           gqa_w_qkv, gqa_q_norm_g, gqa_k_norm_g, gqa_w_o,
           conv_w_in, conv_w, conv_w_out, final_g):
    b, s, d = x.shape
    depth = mod_w.shape[0]
    diff_head_dim = diff_lambda.shape[1]
    diff_tables = _diff_rope_tables(s, diff_head_dim, diff_head_dim // 4)
    axial_tables = _axial_rope_tables(s, gqa_q_norm_g.shape[0] // 2)

    mods = _mod_call(c, mod_w, mod_b).reshape(depth, b, N_MOD, d)
    w_gu, w_down = ffn_w_gu.astype(BF16), ffn_w_down.astype(BF16)

    for i in range(depth):
        m_ffn0, m_mix, m_ffn1 = (mods[i, :, 3 * k:3 * k + 3, :] for k in range(3))
        x = _ffn_call(x, m_ffn0, norm_g[i, 0], w_gu, w_down, i, 0, final_g)
        kind = i % 4
        attn = None
        if kind == 0:
            x = _pool_call(x, m_mix, norm_g[i, 1], pool_w.astype(BF16), pool_scale)
        elif kind == 1:
            q, k, vt = _diff_qkv_call(x, m_mix, norm_g[i, 1], diff_w_qkv.astype(BF16), diff_tables,
                                      diff_head_dim, diff_head_dim // 4)
            attn = (_diff_flash_call(q, k, vt, diff_lambda, diff_subln_g, i), m_mix, diff_w_o.astype(BF16))
        elif kind == 2:
            q, k, vt = _gqa_qkv_call(x, m_mix, norm_g[i, 1], gqa_w_qkv.astype(BF16),
                                     gqa_q_norm_g, gqa_k_norm_g, axial_tables)
            attn = (_gqa_flash_call(q, k, vt), m_mix, gqa_w_o.astype(BF16))
        else:
            x = _conv_call(x, m_mix, norm_g[i, 1], conv_w_in.astype(BF16), conv_w, conv_w_out.astype(BF16))
        x = _ffn_call(x, m_ffn1, norm_g[i, 2], w_gu, w_down, i, 1, final_g, final=(i == depth - 1), attn=attn)

    return x
```

```python
import functools
import math

import jax
import jax.numpy as jnp
from jax import lax
from jax.experimental import pallas as pl
from jax.experimental.pallas import tpu as pltpu

F32 = jnp.float32
BF16 = jnp.bfloat16

LANES = 128
SUBLANES = 8
VMEM_LIMIT_BYTES = 56 * 1024 * 1024

EPS = 1e-6
LOG2_E = math.log2(math.e)
GRID_W = 64
POOL_WINDOWS = (2, 4, 8, 16)
DIFF_HEADS = 8
ROPE_THETA = 500000.0
GQA_HEADS = 8
GQA_KV_HEADS = 2
AXIAL_THETA = 10000.0
N_MOD = 9

TOKEN_TILE = 512
FFN_CHUNK = 256
Q_TILE = 2048
GQA_Q_TILE = 1024
KV_TILE = 512
HALO = SUBLANES


def _params(*sem):
    return pltpu.CompilerParams(dimension_semantics=sem, vmem_limit_bytes=VMEM_LIMIT_BYTES)


def _silu(x):
    return x * jax.nn.sigmoid(x)


def _norm_mod(x, g, m_ref):
    ms = jnp.mean(x * x, axis=-1, keepdims=True)
    y = x * lax.rsqrt(ms + EPS) * g
    return y * (1.0 + m_ref[1:2, :]) + m_ref[0:1, :]


def _dot(a, b):
    return jnp.dot(a, b, preferred_element_type=F32)


def _mod_kernel(c_ref, w_ref, b_ref, o_ref):
    ca = _silu(c_ref[...]).astype(BF16)
    o_ref[...] = _dot(ca, w_ref[...].astype(BF16)) + b_ref[...]


def _mod_call(c, mod_w, mod_b):
    depth, d, nd = mod_w.shape
    b = c.shape[0]
    tn = d
    return pl.pallas_call(
        _mod_kernel,
        grid=(depth, nd // tn),
        in_specs=[
            pl.BlockSpec((b, d), lambda l, j: (0, 0)),
            pl.BlockSpec((None, d, tn), lambda l, j: (l, 0, j)),
            pl.BlockSpec((None, 1, tn), lambda l, j: (l, 0, j)),
        ],
        out_specs=pl.BlockSpec((None, b, tn), lambda l, j: (l, 0, j)),
        out_shape=jax.ShapeDtypeStruct((depth, b, nd), F32),
        compiler_params=_params("parallel", "parallel"),
        name="adaln_mod",
    )(c, mod_w, mod_b.reshape(depth, 1, nd))


def _ffn_body(x, m_ref, g_ref, wgu_ref, wd_ref, fg_ref, o_ref, act_ref, *, d_ff, chunk, final):
    h = _norm_mod(x, g_ref[...], m_ref).astype(BF16)
    for c0 in range(0, d_ff, chunk):
        gate = _dot(h, wgu_ref[:, c0:c0 + chunk])
        up = _dot(h, wgu_ref[:, d_ff + c0:d_ff + c0 + chunk])
        act_ref[:, c0:c0 + chunk] = (_silu(gate) * up).astype(BF16)
    y = _dot(act_ref[...], wd_ref[...])
    out = x + (0.5 * m_ref[2:3, :]) * y
    if final:
        ms = jnp.mean(out * out, axis=-1, keepdims=True)
        out = out * lax.rsqrt(ms + EPS) * fg_ref[...]
    o_ref[...] = out


def _ffn_kernel(x_ref, m_ref, g_ref, wgu_ref, wd_ref, fg_ref, o_ref, act_ref, **kw):
    _ffn_body(x_ref[...], m_ref, g_ref, wgu_ref, wd_ref, fg_ref, o_ref, act_ref, **kw)


def _attn_ffn_kernel(x_ref, a_ref, mm_ref, wo_ref, m_ref, g_ref, wgu_ref, wd_ref, fg_ref, o_ref, act_ref, **kw):
    x = x_ref[...] + mm_ref[2:3, :] * _dot(a_ref[...], wo_ref[...])
    _ffn_body(x, m_ref, g_ref, wgu_ref, wd_ref, fg_ref, o_ref, act_ref, **kw)


def _ffn_call(x, m3, g, w_gu, w_down, layer, which, final_g, *, final=False, attn=None):
    b, s, d = x.shape
    d_ff = w_down.shape[2]
    tm = min(TOKEN_TILE, s)
    chunk = FFN_CHUNK if d_ff % FFN_CHUNK == 0 else d_ff
    tok = pl.BlockSpec((None, tm, d), lambda bi, i: (bi, i, 0))
    mod = pl.BlockSpec((None, 3, d), lambda bi, i: (bi, 0, 0))
    row = pl.BlockSpec((1, d), lambda bi, i: (0, 0))
    in_specs = [mod, row,
                pl.BlockSpec((None, None, d, 2 * d_ff), lambda bi, i: (layer, which, 0, 0)),
                pl.BlockSpec((None, None, d_ff, d), lambda bi, i: (layer, which, 0, 0)),
                row]
    args = [m3, g.reshape(1, d), w_gu, w_down, final_g.reshape(1, d)]
    if attn is None:
        body, in_specs, args = _ffn_kernel, [tok] + in_specs, [x] + args
    else:
        a, m_mix, w_o = attn
        body = _attn_ffn_kernel
        in_specs = [tok, tok, mod, pl.BlockSpec((d, d), lambda bi, i: (0, 0))] + in_specs
        args = [x, a, m_mix, w_o] + args
    return pl.pallas_call(
        functools.partial(body, d_ff=d_ff, chunk=chunk, final=final),
        grid=(b, s // tm),
        in_specs=in_specs,
        out_specs=tok,
        out_shape=jax.ShapeDtypeStruct((b, s, d), F32),
        scratch_shapes=[pltpu.VMEM((tm, d_ff), BF16)],
        compiler_params=_params("parallel", "parallel"),
        name="ffn",
    )(*args)


def _halo_specs(tm, d, s):
    per = tm // HALO
    last = s // HALO - 1
    return [
        pl.BlockSpec((None, tm, d), lambda bi, i: (bi, i, 0)),
        pl.BlockSpec((None, HALO, d), lambda bi, i: (bi, jnp.maximum(i * per - 1, 0), 0)),
        pl.BlockSpec((None, HALO, d), lambda bi, i: (bi, jnp.minimum((i + 1) * per, last), 0)),
    ]


def _fill_halo_scratch(hs_ref, x_ref, xp_ref, xn_ref, m_ref, g, tm):
    i = pl.program_id(1)
    n = pl.num_programs(1)
    h = _norm_mod(x_ref[...], g, m_ref)
    hp = _norm_mod(xp_ref[...], g, m_ref)
    hn = _norm_mod(xn_ref[...], g, m_ref)
    hs_ref[0:HALO, :] = jnp.where(i > 0, hp, 0.0)
    hs_ref[HALO:HALO + tm, :] = h
    hs_ref[HALO + tm:HALO + tm + HALO, :] = jnp.where(i < n - 1, hn, 0.0)
    return h


def _pool_kernel(x_ref, xp_ref, xn_ref, m_ref, g_ref, pw_ref, ps_ref, o_ref, hs_ref, *, seq, tm, group):
    i = pl.program_id(1)
    h = _fill_halo_scratch(hs_ref, x_ref, xp_ref, xn_ref, m_ref, g_ref[...], tm)
    t_abs = i * tm + lax.broadcasted_iota(jnp.int32, (tm, 1), 0)
    for gi, win in enumerate(POOL_WINDOWS):
        half = win // 2
        c0, c1 = gi * group, (gi + 1) * group
        acc = hs_ref[HALO - half:HALO - half + tm, c0:c1]
        for k in range(-half + 1, half):
            acc = acc + hs_ref[HALO + k:HALO + k + tm, c0:c1]
        lo = jnp.maximum(t_abs - half, 0)
        hi = jnp.minimum(t_abs + half, seq)
        cnt = (hi - lo).astype(F32)
        diff = acc / cnt - h[:, c0:c1]
        y = _dot(diff.astype(BF16), pw_ref[gi]) * ps_ref[:, c0:c1]
        o_ref[:, c0:c1] = x_ref[:, c0:c1] + m_ref[2:3, c0:c1] * y


def _pool_call(x, m3, g, pool_w, pool_scale):
    b, s, d = x.shape
    n_groups, group, _ = pool_w.shape
    assert max(POOL_WINDOWS) // 2 <= HALO
    tm = min(TOKEN_TILE, s)
    kern = functools.partial(_pool_kernel, seq=s, tm=tm, group=group)
    return pl.pallas_call(
        kern,
        grid=(b, s // tm),
        in_specs=_halo_specs(tm, d, s) + [
            pl.BlockSpec((None, 3, d), lambda bi, i: (bi, 0, 0)),
            pl.BlockSpec((1, d), lambda bi, i: (0, 0)),
            pl.BlockSpec((n_groups, group, group), lambda bi, i: (0, 0, 0)),
            pl.BlockSpec((1, d), lambda bi, i: (0, 0)),
        ],
        out_specs=pl.BlockSpec((None, tm, d), lambda bi, i: (bi, i, 0)),
        out_shape=jax.ShapeDtypeStruct((b, s, d), F32),
        scratch_shapes=[pltpu.VMEM((tm + 2 * HALO, d), F32)],
        compiler_params=_params("parallel", "parallel"),
        name="pool_mixer",
    )(x, x, x, m3, g.reshape(1, d), pool_w, pool_scale.reshape(1, d))


def _conv_kernel(x_ref, xp_ref, xn_ref, m_ref, g_ref, win_ref, cw_ref, wout_ref, o_ref, hs_ref, zs_ref, *, tm, d):
    _fill_halo_scratch(hs_ref, x_ref, xp_ref, xn_ref, m_ref, g_ref[...], tm)
    proj = _dot(hs_ref[...].astype(BF16), win_ref[...])
    gb = proj[HALO:HALO + tm, 0:d]
    zs_ref[...] = proj[:, d:2 * d] * proj[:, 2 * d:3 * d]
    zc = (cw_ref[0:1, :] * zs_ref[HALO - 1:HALO - 1 + tm, :]
          + cw_ref[1:2, :] * zs_ref[HALO:HALO + tm, :]
          + cw_ref[2:3, :] * zs_ref[HALO + 1:HALO + 1 + tm, :])
    y = _dot((gb * zc).astype(BF16), wout_ref[...])
    o_ref[...] = x_ref[...] + m_ref[2:3, :] * y


def _conv_call(x, m3, g, w_in, conv_w, w_out):
    b, s, d = x.shape
    assert conv_w.shape[0] == 3
    tm = min(TOKEN_TILE, s)
    kern = functools.partial(_conv_kernel, tm=tm, d=d)
    return pl.pallas_call(
        kern,
        grid=(b, s // tm),
        in_specs=_halo_specs(tm, d, s) + [
            pl.BlockSpec((None, 3, d), lambda bi, i: (bi, 0, 0)),
            pl.BlockSpec((1, d), lambda bi, i: (0, 0)),
            pl.BlockSpec((d, 3 * d), lambda bi, i: (0, 0)),
            pl.BlockSpec((3, d), lambda bi, i: (0, 0)),
            pl.BlockSpec((d, d), lambda bi, i: (0, 0)),
        ],
        out_specs=pl.BlockSpec((None, tm, d), lambda bi, i: (bi, i, 0)),
        out_shape=jax.ShapeDtypeStruct((b, s, d), F32),
        scratch_shapes=[pltpu.VMEM((tm + 2 * HALO, d), F32), pltpu.VMEM((tm + 2 * HALO, d), F32)],
        compiler_params=_params("parallel", "parallel"),
        name="conv_mixer",
    )(x, x, x, m3, g.reshape(1, d), w_in, conv_w, w_out)


def _rope_angles(pos, dim, theta):
    inv = 1.0 / (theta ** (jnp.arange(0, dim, 2, dtype=F32) / dim))
    ang = pos.astype(F32)[:, None] * inv[None, :]
    return jnp.cos(ang), jnp.sin(ang)


def _diff_rope_tables(s, head_dim, rot_dim):
    cos, sin = _rope_angles(jnp.arange(s), rot_dim, ROPE_THETA)
    half = rot_dim // 2
    ones = jnp.ones((s, head_dim - rot_dim), F32)
    zeros = jnp.zeros((s, head_dim - rot_dim), F32)
    zh = jnp.zeros((s, half), F32)
    c = jnp.concatenate([cos, cos, ones], axis=1)
    s1 = jnp.concatenate([zh, sin, zeros], axis=1)
    s2 = jnp.concatenate([-sin, zh, zeros], axis=1)
    reps = LANES // head_dim
    return tuple(jnp.tile(t, (1, reps)) for t in (c, s1, s2))


def _axial_rope_tables(s, axial_dim):
    rows = s // GRID_W
    row_pos = jnp.broadcast_to(jnp.arange(rows)[:, None], (rows, GRID_W)).reshape(-1)
    col_pos = jnp.broadcast_to(jnp.arange(GRID_W)[None, :], (rows, GRID_W)).reshape(-1)
    cr, sr = _rope_angles(row_pos, axial_dim, AXIAL_THETA)
    cc, sc = _rope_angles(col_pos, axial_dim, AXIAL_THETA)
    z = jnp.zeros_like(sr)
    c = jnp.concatenate([cr, cr, cc, cc], axis=1)
    s1 = jnp.concatenate([z, sr, z, sc], axis=1)
    s2 = jnp.concatenate([-sr, z, -sc, z], axis=1)
    return c, s1, s2


def _rotate(t, shift, c, s1, s2):
    return t * c + pltpu.roll(t, shift, 1) * s1 + pltpu.roll(t, LANES - shift, 1) * s2


def _diff_qkv_kernel(x_ref, m_ref, g_ref, w_ref, c_ref, s1_ref, s2_ref, q_ref, k_ref, v_ref, *, d, shift, scale):
    h = _norm_mod(x_ref[...], g_ref[...], m_ref).astype(BF16)
    proj = _dot(h, w_ref[...])
    c, s1, s2 = c_ref[...], s1_ref[...], s2_ref[...]
    for j in range(d // LANES):
        lo, hi = j * LANES, (j + 1) * LANES
        q_ref[:, lo:hi] = (_rotate(proj[:, lo:hi], shift, c, s1, s2) * scale).astype(BF16)
        k_ref[:, lo:hi] = _rotate(proj[:, d + lo:d + hi], shift, c, s1, s2).astype(BF16)
    v_ref[...] = proj[:, 2 * d:3 * d].T.astype(BF16)


def _diff_qkv_call(x, m3, g, w_qkv, tables, head_dim, rot_dim):
    b, s, d = x.shape
    tm = min(TOKEN_TILE, s)
    kern = functools.partial(_diff_qkv_kernel, d=d, shift=rot_dim // 2, scale=head_dim ** -0.5 * LOG2_E)
    tok = pl.BlockSpec((None, tm, d), lambda bi, i: (bi, i, 0))
    tab = pl.BlockSpec((tm, LANES), lambda bi, i: (i, 0))
    out = jax.ShapeDtypeStruct((b, s, d), BF16)
    return pl.pallas_call(
        kern,
        grid=(b, s // tm),
        in_specs=[
            tok,
            pl.BlockSpec((None, 3, d), lambda bi, i: (bi, 0, 0)),
            pl.BlockSpec((1, d), lambda bi, i: (0, 0)),
            pl.BlockSpec((d, 3 * d), lambda bi, i: (0, 0)),
            tab, tab, tab,
        ],
        out_specs=[tok, tok, pl.BlockSpec((None, d, tm), lambda bi, i: (bi, 0, i))],
        out_shape=[out, out, jax.ShapeDtypeStruct((b, d, s), BF16)],
        compiler_params=_params("parallel", "parallel"),
        name="diff_qkv",
    )(x, m3, g.reshape(1, d), w_qkv, *tables)


def _gqa_qkv_kernel(x_ref, m_ref, g_ref, w_ref, qg_ref, kg_ref, c_ref, s1_ref, s2_ref,
                    q_ref, k_ref, v_ref, *, n_q, n_kv, shift, scale):
    h = _norm_mod(x_ref[...], g_ref[...], m_ref).astype(BF16)
    proj = _dot(h, w_ref[...])
    c, s1, s2 = c_ref[...], s1_ref[...], s2_ref[...]

    def head(j, gain):
        t = proj[:, j * LANES:(j + 1) * LANES]
        ms = jnp.mean(t * t, axis=-1, keepdims=True)
        return _rotate(t * lax.rsqrt(ms + EPS) * gain, shift, c, s1, s2)

    for j in range(n_q):
        q_ref[:, j * LANES:(j + 1) * LANES] = (head(j, qg_ref[...]) * scale).astype(BF16)
    for j in range(n_kv):
        k_ref[:, j * LANES:(j + 1) * LANES] = head(n_q + j, kg_ref[...]).astype(BF16)
    v_ref[...] = proj[:, (n_q + n_kv) * LANES:].T.astype(BF16)


def _gqa_qkv_call(x, m3, g, w_qkv, q_gain, k_gain, tables):
    b, s, d = x.shape
    hd = q_gain.shape[0]
    assert hd == LANES
    n_q, n_kv = GQA_HEADS, GQA_KV_HEADS
    tm = min(TOKEN_TILE, s)
    kern = functools.partial(_gqa_qkv_kernel, n_q=n_q, n_kv=n_kv, shift=hd // 4, scale=hd ** -0.5 * LOG2_E)
    tab = pl.BlockSpec((tm, LANES), lambda bi, i: (i, 0))

    def tok(width):
        return pl.BlockSpec((None, tm, width), lambda bi, i: (bi, i, 0))

    return pl.pallas_call(
        kern,
        grid=(b, s // tm),
        in_specs=[
            tok(d),
            pl.BlockSpec((None, 3, d), lambda bi, i: (bi, 0, 0)),
            pl.BlockSpec((1, d), lambda bi, i: (0, 0)),
            pl.BlockSpec(w_qkv.shape, lambda bi, i: (0, 0)),
            pl.BlockSpec((1, hd), lambda bi, i: (0, 0)),
            pl.BlockSpec((1, hd), lambda bi, i: (0, 0)),
            tab, tab, tab,
        ],
        out_specs=[tok(n_q * hd), tok(n_kv * hd),
                   pl.BlockSpec((None, n_kv * hd, tm), lambda bi, i: (bi, 0, i))],
        out_shape=[jax.ShapeDtypeStruct((b, s, n_q * hd), BF16),
                   jax.ShapeDtypeStruct((b, s, n_kv * hd), BF16),
                   jax.ShapeDtypeStruct((b, n_kv * hd, s), BF16)],
        compiler_params=_params("parallel", "parallel"),
        name="gqa_qkv",
    )(x, m3, g.reshape(1, d), w_qkv, q_gain.reshape(1, hd), k_gain.reshape(1, hd), *tables)


def _flash_sweep(qs_ref, k_ref, vt_ref, scratch, *, seq, tk):
    s_refs, p_refs, m_ref, l_ref, a_ref, acc_ref = scratch[0:2], scratch[2:4], *scratch[4:8]
    n = seq // tk
    assert n % 2 == 0 and n * tk == seq

    def slab(j):
        return pl.ds(pl.multiple_of(j * tk, tk), tk)

    def scores_into(s_ref, j):
        s_ref[...] = lax.dot_general(k_ref[slab(j), :], qs_ref[...], (((1,), (1,)), ((), ())),
                                     preferred_element_type=F32)

    def accumulate(p_ref, j):
        acc_ref[...] = a_ref[...] * acc_ref[...] + _dot(vt_ref[:, slab(j)], p_ref[...])

    def softmax_into(p_ref, s_ref):
        s = s_ref[...]
        m_prev = m_ref[...]
        m_next = jnp.maximum(m_prev, jnp.max(s, axis=0, keepdims=True))
        alpha = jnp.exp2(m_prev - m_next)
        p = jnp.exp2(s - m_next)
        m_ref[...] = m_next
        l_ref[...] = alpha * l_ref[...] + jnp.sum(p, axis=0, keepdims=True)
        p_ref[...] = p.astype(BF16)
        return alpha

    def one_pass(s_cur, s_nxt, p_cur, p_prv, j):
        scores_into(s_nxt, jnp.minimum(j + 1, n - 1))
        accumulate(p_prv, jnp.maximum(j - 1, 0))
        a_ref[...] = softmax_into(p_cur, s_cur)

    m_ref[...] = jnp.full(m_ref.shape, -jnp.inf, F32)
    l_ref[...] = jnp.zeros(l_ref.shape, F32)
    a_ref[...] = jnp.zeros(a_ref.shape, F32)
    acc_ref[...] = jnp.zeros(acc_ref.shape, F32)
    p_refs[1][...] = jnp.zeros(p_refs[1].shape, BF16)
    scores_into(s_refs[0], 0)

    def step(i, carry):
        one_pass(s_refs[0], s_refs[1], p_refs[0], p_refs[1], 2 * i)
        one_pass(s_refs[1], s_refs[0], p_refs[1], p_refs[0], 2 * i + 1)
        return carry

    lax.fori_loop(0, n // 2, step, 0)
    accumulate(p_refs[1], n - 1)
    return acc_ref[...] / l_ref[...]


def _flash_scratch(cols, tk, v_dim):
    return [pltpu.VMEM((cols, LANES), BF16),
            pltpu.VMEM((tk, cols), F32), pltpu.VMEM((tk, cols), F32),
            pltpu.VMEM((tk, cols), BF16), pltpu.VMEM((tk, cols), BF16),
            pltpu.VMEM((1, cols), F32), pltpu.VMEM((1, cols), F32), pltpu.VMEM((1, cols), F32),
            pltpu.VMEM((v_dim, cols), F32)]


def _kv_tile(s):
    tk = min(KV_TILE, s // 2)
    assert s % (2 * tk) == 0
    return tk


def _diff_flash_kernel(q_ref, k_ref, vt_ref, lam_ref, sg_ref, o_ref, qs_ref, *scratch,
                       seq, tq, tk, half, lam_init):
    q = q_ref[...]
    lane = lax.broadcasted_iota(jnp.int32, q.shape, 1)
    zero = jnp.zeros_like(q)
    qs_ref[0:tq, :] = jnp.where(lane < half, q, zero)
    qs_ref[tq:2 * tq, :] = jnp.where(lane >= half, q, zero)
    ont = _flash_sweep(qs_ref, k_ref, vt_ref, scratch, seq=seq, tk=tk)
    lf = lam_ref[...]
    lam = (jnp.exp(jnp.sum(lf[0:1, :] * lf[1:2, :], axis=1, keepdims=True))
           - jnp.exp(jnp.sum(lf[2:3, :] * lf[3:4, :], axis=1, keepdims=True)) + lam_init)
    o = (ont[:, 0:tq] - lam * ont[:, tq:2 * tq]).T
    ms = jnp.mean(o * o, axis=-1, keepdims=True)
    o_ref[...] = ((o * lax.rsqrt(ms + EPS) * sg_ref[...]) * (1.0 - lam_init)).astype(BF16)


def _diff_flash_call(q, k, vt, lam, subln_g, layer_idx):
    b, s, d = q.shape
    v_dim = subln_g.shape[0]
    assert v_dim == LANES and d == DIFF_HEADS * v_dim
    tq, tk = min(Q_TILE, s), _kv_tile(s)
    lam_init = 0.8 - 0.6 * math.exp(-0.3 * layer_idx)
    kern = functools.partial(_diff_flash_kernel, seq=s, tq=tq, tk=tk, half=v_dim // 2, lam_init=lam_init)
    qo = pl.BlockSpec((None, tq, v_dim), lambda bi, h, i: (bi, i, h))
    return pl.pallas_call(
        kern,
        grid=(b, DIFF_HEADS, s // tq),
        in_specs=[qo,
                  pl.BlockSpec((None, s, v_dim), lambda bi, h, i: (bi, 0, h)),
                  pl.BlockSpec((None, v_dim, s), lambda bi, h, i: (bi, h, 0)),
                  pl.BlockSpec(lam.shape, lambda bi, h, i: (0, 0)),
                  pl.BlockSpec((1, v_dim), lambda bi, h, i: (0, 0))],
        out_specs=qo,
        out_shape=jax.ShapeDtypeStruct((b, s, d), BF16),
        scratch_shapes=_flash_scratch(2 * tq, tk, v_dim),
        compiler_params=_params("parallel", "parallel", "parallel"),
        name="diff_flash",
    )(q, k, vt, lam, subln_g.reshape(1, v_dim))


def _gqa_flash_kernel(q_ref, k_ref, vt_ref, o_ref, qs_ref, *scratch, seq, tq, tk, group):
    for r in range(group):
        qs_ref[r * tq:(r + 1) * tq, :] = q_ref[:, r * LANES:(r + 1) * LANES]
    ont = _flash_sweep(qs_ref, k_ref, vt_ref, scratch, seq=seq, tk=tk)
    for r in range(group):
        o_ref[:, r * LANES:(r + 1) * LANES] = ont[:, r * tq:(r + 1) * tq].T.astype(BF16)


def _gqa_flash_call(q, k, vt):
    b, s, d = q.shape
    hd = LANES
    group = GQA_HEADS // GQA_KV_HEADS
    tq, tk = min(GQA_Q_TILE, s), _kv_tile(s)
    kern = functools.partial(_gqa_flash_kernel, seq=s, tq=tq, tk=tk, group=group)
    qo = pl.BlockSpec((None, tq, group * hd), lambda bi, h, i: (bi, i, h))
    return pl.pallas_call(
        kern,
        grid=(b, GQA_KV_HEADS, s // tq),
        in_specs=[qo,
                  pl.BlockSpec((None, s, hd), lambda bi, h, i: (bi, 0, h)),
                  pl.BlockSpec((None, hd, s), lambda bi, h, i: (bi, h, 0))],
        out_specs=qo,
        out_shape=jax.ShapeDtypeStruct((b, s, d), BF16),
        scratch_shapes=_flash_scratch(group * tq, tk, hd),
        compiler_params=_params("parallel", "parallel", "parallel"),
        name="gqa_flash",
    )(q, k, vt)


def kernel(x, c, mod_w, mod_b, norm_g, ffn_w_gu, ffn_w_down, pool_w, pool_scale,
           diff_w_qkv, diff_lambda, diff_subln_g, diff_w_o,
           gqa_w_qkv, gqa_q_norm_g, gqa_k_norm_g, gqa_w_o,
           conv_w_in, conv_w, conv_w_out, final_g):
    b, s, d = x.shape
    depth = mod_w.shape[0]
    diff_head_dim = diff_lambda.shape[1]
    diff_tables = _diff_rope_tables(s, diff_head_dim, diff_head_dim // 4)
    axial_tables = _axial_rope_tables(s, gqa_q_norm_g.shape[0] // 2)

    mods = _mod_call(c, mod_w, mod_b).reshape(depth, b, N_MOD, d)
    w_gu, w_down = ffn_w_gu.astype(BF16), ffn_w_down.astype(BF16)

    for i in range(depth):
        m_ffn0, m_mix, m_ffn1 = (mods[i, :, 3 * k:3 * k + 3, :] for k in range(3))
        x = _ffn_call(x, m_ffn0, norm_g[i, 0], w_gu, w_down, i, 0, final_g)
        kind = i % 4
        attn = None
        if kind == 0:
            x = _pool_call(x, m_mix, norm_g[i, 1], pool_w.astype(BF16), pool_scale)
        elif kind == 1:
            q, k, vt = _diff_qkv_call(x, m_mix, norm_g[i, 1], diff_w_qkv.astype(BF16), diff_tables,
                                      diff_head_dim, diff_head_dim // 4)
            attn = (_diff_flash_call(q, k, vt, diff_lambda, diff_subln_g, i), m_mix, diff_w_o.astype(BF16))
        elif kind == 2:
            q, k, vt = _gqa_qkv_call(x, m_mix, norm_g[i, 1], gqa_w_qkv.astype(BF16),
                                     gqa_q_norm_g, gqa_k_norm_g, axial_tables)
            attn = (_gqa_flash_call(q, k, vt), m_mix, gqa_w_o.astype(BF16))
        else:
            x = _conv_call(x, m_mix, norm_g[i, 1], conv_w_in.astype(BF16), conv_w, conv_w_out.astype(BF16))
        x = _ffn_call(x, m_ffn1, norm_g[i, 2], w_gu, w_down, i, 1, final_g, final=(i == depth - 1), attn=attn)

    return x
```

```python
import functools
import math

import jax
import jax.numpy as jnp
from jax import lax
from jax.experimental import pallas as pl
from jax.experimental.pallas import tpu as pltpu

F32 = jnp.float32
BF16 = jnp.bfloat16

LANES = 128
SUBLANES = 8
VMEM_LIMIT_BYTES = 56 * 1024 * 1024

EPS = 1e-6
LOG2_E = math.log2(math.e)
GRID_W = 64
POOL_WINDOWS = (2, 4, 8, 16)
DIFF_HEADS = 8
ROPE_THETA = 500000.0
GQA_HEADS = 8
GQA_KV_HEADS = 2
AXIAL_THETA = 10000.0
N_MOD = 9

TOKEN_TILE = 512
FFN_CHUNK = 256
Q_TILE = 2048
GQA_Q_TILE = 1024
KV_TILE = 512
FLASH_UNROLL = 2
HALO = SUBLANES


def _params(*sem):
    return pltpu.CompilerParams(dimension_semantics=sem, vmem_limit_bytes=VMEM_LIMIT_BYTES)


def _silu(x):
    return x * jax.nn.sigmoid(x)


def _norm_mod(x, g, m_ref):
    ms = jnp.mean(x * x, axis=-1, keepdims=True)
    y = x * lax.rsqrt(ms + EPS) * g
    return y * (1.0 + m_ref[1:2, :]) + m_ref[0:1, :]


def _dot(a, b):
    return jnp.dot(a, b, preferred_element_type=F32)


def _mod_kernel(c_ref, w_ref, b_ref, o_ref):
    ca = _silu(c_ref[...]).astype(BF16)
    o_ref[...] = _dot(ca, w_ref[...].astype(BF16)) + b_ref[...]


def _mod_call(c, mod_w, mod_b):
    depth, d, nd = mod_w.shape
    b = c.shape[0]
    tn = d
    return pl.pallas_call(
        _mod_kernel,
        grid=(depth, nd // tn),
        in_specs=[
            pl.BlockSpec((b, d), lambda l, j: (0, 0)),
            pl.BlockSpec((None, d, tn), lambda l, j: (l, 0, j)),
            pl.BlockSpec((None, 1, tn), lambda l, j: (l, 0, j)),
        ],
        out_specs=pl.BlockSpec((None, b, tn), lambda l, j: (l, 0, j)),
        out_shape=jax.ShapeDtypeStruct((depth, b, nd), F32),
        compiler_params=_params("parallel", "parallel"),
        name="adaln_mod",
    )(c, mod_w, mod_b.reshape(depth, 1, nd))


def _ffn_body(x, m_ref, g_ref, wgu_ref, wd_ref, fg_ref, o_ref, act_ref, *, d_ff, chunk, final):
    h = _norm_mod(x, g_ref[...], m_ref).astype(BF16)
    for c0 in range(0, d_ff, chunk):
        gate = _dot(h, wgu_ref[:, c0:c0 + chunk])
        up = _dot(h, wgu_ref[:, d_ff + c0:d_ff + c0 + chunk])
        act_ref[:, c0:c0 + chunk] = (_silu(gate) * up).astype(BF16)
    y = _dot(act_ref[...], wd_ref[...])
    out = x + (0.5 * m_ref[2:3, :]) * y
    if final:
        ms = jnp.mean(out * out, axis=-1, keepdims=True)
        out = out * lax.rsqrt(ms + EPS) * fg_ref[...]
    o_ref[...] = out


def _ffn_kernel(x_ref, m_ref, g_ref, wgu_ref, wd_ref, fg_ref, o_ref, act_ref, **kw):
    _ffn_body(x_ref[...], m_ref, g_ref, wgu_ref, wd_ref, fg_ref, o_ref, act_ref, **kw)


def _attn_ffn_kernel(x_ref, a_ref, mm_ref, wo_ref, m_ref, g_ref, wgu_ref, wd_ref, fg_ref, o_ref, act_ref, **kw):
    x = x_ref[...] + mm_ref[2:3, :] * _dot(a_ref[...], wo_ref[...])
    _ffn_body(x, m_ref, g_ref, wgu_ref, wd_ref, fg_ref, o_ref, act_ref, **kw)


def _ffn_call(x, m3, g, w_gu, w_down, layer, which, final_g, *, final=False, attn=None):
    b, s, d = x.shape
    d_ff = w_down.shape[2]
    tm = min(TOKEN_TILE, s)
    chunk = FFN_CHUNK if d_ff % FFN_CHUNK == 0 else d_ff
    tok = pl.BlockSpec((None, tm, d), lambda bi, i: (bi, i, 0))
    mod = pl.BlockSpec((None, 3, d), lambda bi, i: (bi, 0, 0))
    row = pl.BlockSpec((1, d), lambda bi, i: (0, 0))
    in_specs = [mod, row,
                pl.BlockSpec((None, None, d, 2 * d_ff), lambda bi, i: (layer, which, 0, 0)),
                pl.BlockSpec((None, None, d_ff, d), lambda bi, i: (layer, which, 0, 0)),
                row]
    args = [m3, g.reshape(1, d), w_gu, w_down, final_g.reshape(1, d)]
    if attn is None:
        body, in_specs, args = _ffn_kernel, [tok] + in_specs, [x] + args
    else:
        a, m_mix, w_o = attn
        body = _attn_ffn_kernel
        in_specs = [tok, tok, mod, pl.BlockSpec((d, d), lambda bi, i: (0, 0))] + in_specs
        args = [x, a, m_mix, w_o] + args
    return pl.pallas_call(
        functools.partial(body, d_ff=d_ff, chunk=chunk, final=final),
        grid=(b, s // tm),
        in_specs=in_specs,
        out_specs=tok,
        out_shape=jax.ShapeDtypeStruct((b, s, d), F32),
        scratch_shapes=[pltpu.VMEM((tm, d_ff), BF16)],
        compiler_params=_params("parallel", "parallel"),
        name="ffn",
    )(*args)


def _halo_specs(tm, d, s):
    per = tm // HALO
    last = s // HALO - 1
    return [
        pl.BlockSpec((None, tm, d), lambda bi, i: (bi, i, 0)),
        pl.BlockSpec((None, HALO, d), lambda bi, i: (bi, jnp.maximum(i * per - 1, 0), 0)),
        pl.BlockSpec((None, HALO, d), lambda bi, i: (bi, jnp.minimum((i + 1) * per, last), 0)),
    ]


def _fill_halo_scratch(hs_ref, x_ref, xp_ref, xn_ref, m_ref, g, tm):
    i = pl.program_id(1)
    n = pl.num_programs(1)
    h = _norm_mod(x_ref[...], g, m_ref)
    hp = _norm_mod(xp_ref[...], g, m_ref)
    hn = _norm_mod(xn_ref[...], g, m_ref)
    hs_ref[0:HALO, :] = jnp.where(i > 0, hp, 0.0)
    hs_ref[HALO:HALO + tm, :] = h
    hs_ref[HALO + tm:HALO + tm + HALO, :] = jnp.where(i < n - 1, hn, 0.0)
    return h


def _pool_kernel(x_ref, xp_ref, xn_ref, m_ref, g_ref, pw_ref, ps_ref, o_ref, hs_ref, *, seq, tm, group):
    i = pl.program_id(1)
    h = _fill_halo_scratch(hs_ref, x_ref, xp_ref, xn_ref, m_ref, g_ref[...], tm)
    t_abs = i * tm + lax.broadcasted_iota(jnp.int32, (tm, 1), 0)
    for gi, win in enumerate(POOL_WINDOWS):
        half = win // 2
        c0, c1 = gi * group, (gi + 1) * group
        acc = hs_ref[HALO - half:HALO - half + tm, c0:c1]
        for k in range(-half + 1, half):
            acc = acc + hs_ref[HALO + k:HALO + k + tm, c0:c1]
        lo = jnp.maximum(t_abs - half, 0)
        hi = jnp.minimum(t_abs + half, seq)
        cnt = (hi - lo).astype(F32)
        diff = acc / cnt - h[:, c0:c1]
        y = _dot(diff.astype(BF16), pw_ref[gi]) * ps_ref[:, c0:c1]
        o_ref[:, c0:c1] = x_ref[:, c0:c1] + m_ref[2:3, c0:c1] * y


def _pool_call(x, m3, g, pool_w, pool_scale):
    b, s, d = x.shape
    n_groups, group, _ = pool_w.shape
    assert max(POOL_WINDOWS) // 2 <= HALO
    tm = min(TOKEN_TILE, s)
    kern = functools.partial(_pool_kernel, seq=s, tm=tm, group=group)
    return pl.pallas_call(
        kern,
        grid=(b, s // tm),
        in_specs=_halo_specs(tm, d, s) + [
            pl.BlockSpec((None, 3, d), lambda bi, i: (bi, 0, 0)),
            pl.BlockSpec((1, d), lambda bi, i: (0, 0)),
            pl.BlockSpec((n_groups, group, group), lambda bi, i: (0, 0, 0)),
            pl.BlockSpec((1, d), lambda bi, i: (0, 0)),
        ],
        out_specs=pl.BlockSpec((None, tm, d), lambda bi, i: (bi, i, 0)),
        out_shape=jax.ShapeDtypeStruct((b, s, d), F32),
        scratch_shapes=[pltpu.VMEM((tm + 2 * HALO, d), F32)],
        compiler_params=_params("parallel", "parallel"),
        name="pool_mixer",
    )(x, x, x, m3, g.reshape(1, d), pool_w, pool_scale.reshape(1, d))


def _conv_kernel(x_ref, xp_ref, xn_ref, m_ref, g_ref, win_ref, cw_ref, wout_ref, o_ref, hs_ref, zs_ref, *, tm, d):
    _fill_halo_scratch(hs_ref, x_ref, xp_ref, xn_ref, m_ref, g_ref[...], tm)
    proj = _dot(hs_ref[...].astype(BF16), win_ref[...])
    gb = proj[HALO:HALO + tm, 0:d]
    zs_ref[...] = proj[:, d:2 * d] * proj[:, 2 * d:3 * d]
    zc = (cw_ref[0:1, :] * zs_ref[HALO - 1:HALO - 1 + tm, :]
          + cw_ref[1:2, :] * zs_ref[HALO:HALO + tm, :]
          + cw_ref[2:3, :] * zs_ref[HALO + 1:HALO + 1 + tm, :])
    y = _dot((gb * zc).astype(BF16), wout_ref[...])
    o_ref[...] = x_ref[...] + m_ref[2:3, :] * y


def _conv_call(x, m3, g, w_in, conv_w, w_out):
    b, s, d = x.shape
    assert conv_w.shape[0] == 3
    tm = min(TOKEN_TILE, s)
    kern = functools.partial(_conv_kernel, tm=tm, d=d)
    return pl.pallas_call(
        kern,
        grid=(b, s // tm),
        in_specs=_halo_specs(tm, d, s) + [
            pl.BlockSpec((None, 3, d), lambda bi, i: (bi, 0, 0)),
            pl.BlockSpec((1, d), lambda bi, i: (0, 0)),
            pl.BlockSpec((d, 3 * d), lambda bi, i: (0, 0)),
            pl.BlockSpec((3, d), lambda bi, i: (0, 0)),
            pl.BlockSpec((d, d), lambda bi, i: (0, 0)),
        ],
        out_specs=pl.BlockSpec((None, tm, d), lambda bi, i: (bi, i, 0)),
        out_shape=jax.ShapeDtypeStruct((b, s, d), F32),
        scratch_shapes=[pltpu.VMEM((tm + 2 * HALO, d), F32), pltpu.VMEM((tm + 2 * HALO, d), F32)],
        compiler_params=_params("parallel", "parallel"),
        name="conv_mixer",
    )(x, x, x, m3, g.reshape(1, d), w_in, conv_w, w_out)


def _rope_angles(pos, dim, theta):
    inv = 1.0 / (theta ** (jnp.arange(0, dim, 2, dtype=F32) / dim))
    ang = pos.astype(F32)[:, None] * inv[None, :]
    return jnp.cos(ang), jnp.sin(ang)


def _diff_rope_tables(s, head_dim, rot_dim):
    cos, sin = _rope_angles(jnp.arange(s), rot_dim, ROPE_THETA)
    half = rot_dim // 2
    ones = jnp.ones((s, head_dim - rot_dim), F32)
    zeros = jnp.zeros((s, head_dim - rot_dim), F32)
    zh = jnp.zeros((s, half), F32)
    c = jnp.concatenate([cos, cos, ones], axis=1)
    s1 = jnp.concatenate([zh, sin, zeros], axis=1)
    s2 = jnp.concatenate([-sin, zh, zeros], axis=1)
    reps = LANES // head_dim
    return tuple(jnp.tile(t, (1, reps)) for t in (c, s1, s2))


def _axial_rope_tables(s, axial_dim):
    rows = s // GRID_W
    row_pos = jnp.broadcast_to(jnp.arange(rows)[:, None], (rows, GRID_W)).reshape(-1)
    col_pos = jnp.broadcast_to(jnp.arange(GRID_W)[None, :], (rows, GRID_W)).reshape(-1)
    cr, sr = _rope_angles(row_pos, axial_dim, AXIAL_THETA)
    cc, sc = _rope_angles(col_pos, axial_dim, AXIAL_THETA)
    z = jnp.zeros_like(sr)
    c = jnp.concatenate([cr, cr, cc, cc], axis=1)
    s1 = jnp.concatenate([z, sr, z, sc], axis=1)
    s2 = jnp.concatenate([-sr, z, -sc, z], axis=1)
    return c, s1, s2


def _rotate(t, shift, c, s1, s2):
    return t * c + pltpu.roll(t, shift, 1) * s1 + pltpu.roll(t, LANES - shift, 1) * s2


def _diff_qkv_kernel(x_ref, m_ref, g_ref, w_ref, c_ref, s1_ref, s2_ref, q_ref, k_ref, v_ref, *, d, shift, scale):
    h = _norm_mod(x_ref[...], g_ref[...], m_ref).astype(BF16)
    proj = _dot(h, w_ref[...])
    c, s1, s2 = c_ref[...], s1_ref[...], s2_ref[...]
    for j in range(d // LANES):
        lo, hi = j * LANES, (j + 1) * LANES
        q_ref[:, lo:hi] = (_rotate(proj[:, lo:hi], shift, c, s1, s2) * scale).astype(BF16)
        k_ref[:, lo:hi] = _rotate(proj[:, d + lo:d + hi], shift, c, s1, s2).astype(BF16)
    v_ref[...] = proj[:, 2 * d:3 * d].T.astype(BF16)


def _diff_qkv_call(x, m3, g, w_qkv, tables, head_dim, rot_dim):
    b, s, d = x.shape
    tm = min(TOKEN_TILE, s)
    kern = functools.partial(_diff_qkv_kernel, d=d, shift=rot_dim // 2, scale=head_dim ** -0.5 * LOG2_E)
    tok = pl.BlockSpec((None, tm, d), lambda bi, i: (bi, i, 0))
    tab = pl.BlockSpec((tm, LANES), lambda bi, i: (i, 0))
    out = jax.ShapeDtypeStruct((b, s, d), BF16)
    return pl.pallas_call(
        kern,
        grid=(b, s // tm),
        in_specs=[
            tok,
            pl.BlockSpec((None, 3, d), lambda bi, i: (bi, 0, 0)),
            pl.BlockSpec((1, d), lambda bi, i: (0, 0)),
            pl.BlockSpec((d, 3 * d), lambda bi, i: (0, 0)),
            tab, tab, tab,
        ],
        out_specs=[tok, tok, pl.BlockSpec((None, d, tm), lambda bi, i: (bi, 0, i))],
        out_shape=[out, out, jax.ShapeDtypeStruct((b, d, s), BF16)],
        compiler_params=_params("parallel", "parallel"),
        name="diff_qkv",
    )(x, m3, g.reshape(1, d), w_qkv, *tables)


def _gqa_qkv_kernel(x_ref, m_ref, g_ref, w_ref, qg_ref, kg_ref, c_ref, s1_ref, s2_ref,
                    q_ref, k_ref, v_ref, *, n_q, n_kv, shift, scale):
    h = _norm_mod(x_ref[...], g_ref[...], m_ref).astype(BF16)
    proj = _dot(h, w_ref[...])
    c, s1, s2 = c_ref[...], s1_ref[...], s2_ref[...]

    def head(j, gain):
        t = proj[:, j * LANES:(j + 1) * LANES]
        ms = jnp.mean(t * t, axis=-1, keepdims=True)
        return _rotate(t * lax.rsqrt(ms + EPS) * gain, shift, c, s1, s2)

    for j in range(n_q):
        q_ref[:, j * LANES:(j + 1) * LANES] = (head(j, qg_ref[...]) * scale).astype(BF16)
    for j in range(n_kv):
        k_ref[:, j * LANES:(j + 1) * LANES] = head(n_q + j, kg_ref[...]).astype(BF16)
    v_ref[...] = proj[:, (n_q + n_kv) * LANES:].T.astype(BF16)


def _gqa_qkv_call(x, m3, g, w_qkv, q_gain, k_gain, tables):
    b, s, d = x.shape
    hd = q_gain.shape[0]
    assert hd == LANES
    n_q, n_kv = GQA_HEADS, GQA_KV_HEADS
    tm = min(TOKEN_TILE, s)
    kern = functools.partial(_gqa_qkv_kernel, n_q=n_q, n_kv=n_kv, shift=hd // 4, scale=hd ** -0.5 * LOG2_E)
    tab = pl.BlockSpec((tm, LANES), lambda bi, i: (i, 0))

    def tok(width):
        return pl.BlockSpec((None, tm, width), lambda bi, i: (bi, i, 0))

    return pl.pallas_call(
        kern,
        grid=(b, s // tm),
        in_specs=[
            tok(d),
            pl.BlockSpec((None, 3, d), lambda bi, i: (bi, 0, 0)),
            pl.BlockSpec((1, d), lambda bi, i: (0, 0)),
            pl.BlockSpec(w_qkv.shape, lambda bi, i: (0, 0)),
            pl.BlockSpec((1, hd), lambda bi, i: (0, 0)),
            pl.BlockSpec((1, hd), lambda bi, i: (0, 0)),
            tab, tab, tab,
        ],
        out_specs=[tok(n_q * hd), tok(n_kv * hd),
                   pl.BlockSpec((None, n_kv * hd, tm), lambda bi, i: (bi, 0, i))],
        out_shape=[jax.ShapeDtypeStruct((b, s, n_q * hd), BF16),
                   jax.ShapeDtypeStruct((b, s, n_kv * hd), BF16),
                   jax.ShapeDtypeStruct((b, n_kv * hd, s), BF16)],
        compiler_params=_params("parallel", "parallel"),
        name="gqa_qkv",
    )(x, m3, g.reshape(1, d), w_qkv, q_gain.reshape(1, hd), k_gain.reshape(1, hd), *tables)


def _flash_sweep(qs_ref, k_ref, vt_ref, scratch, *, seq, tk):
    s_refs, p_refs, m_ref, l_ref, a_ref, acc_ref = scratch[0:2], scratch[2:4], *scratch[4:8]
    n = seq // tk
    assert n % 2 == 0 and n * tk == seq

    def slab(j):
        return pl.ds(pl.multiple_of(j * tk, tk), tk)

    def scores_into(s_ref, j):
        s_ref[...] = lax.dot_general(k_ref[slab(j), :], qs_ref[...], (((1,), (1,)), ((), ())),
                                     preferred_element_type=F32)

    def accumulate(p_ref, j):
        acc_ref[...] = a_ref[...] * acc_ref[...] + _dot(vt_ref[:, slab(j)], p_ref[...])

    def softmax_into(p_ref, s_ref):
        s = s_ref[...]
        m_prev = m_ref[...]
        m_next = jnp.maximum(m_prev, jnp.max(s, axis=0, keepdims=True))
        alpha = jnp.exp2(m_prev - m_next)
        p = jnp.exp2(s - m_next)
        m_ref[...] = m_next
        l_ref[...] = alpha * l_ref[...] + jnp.sum(p, axis=0, keepdims=True)
        p_ref[...] = p.astype(BF16)
        return alpha

    def one_pass(s_cur, s_nxt, p_cur, p_prv, j):
        scores_into(s_nxt, jnp.minimum(j + 1, n - 1))
        accumulate(p_prv, jnp.maximum(j - 1, 0))
        a_ref[...] = softmax_into(p_cur, s_cur)

    m_ref[...] = jnp.full(m_ref.shape, -jnp.inf, F32)
    l_ref[...] = jnp.zeros(l_ref.shape, F32)
    a_ref[...] = jnp.zeros(a_ref.shape, F32)
    acc_ref[...] = jnp.zeros(acc_ref.shape, F32)
    p_refs[1][...] = jnp.zeros(p_refs[1].shape, BF16)
    scores_into(s_refs[0], 0)

    def step(i, carry):
        one_pass(s_refs[0], s_refs[1], p_refs[0], p_refs[1], 2 * i)
        one_pass(s_refs[1], s_refs[0], p_refs[1], p_refs[0], 2 * i + 1)
        return carry

    lax.fori_loop(0, n // 2, step, 0, unroll=FLASH_UNROLL)
    accumulate(p_refs[1], n - 1)
    return acc_ref[...] / l_ref[...]


def _flash_scratch(cols, tk, v_dim):
    return [pltpu.VMEM((cols, LANES), BF16),
            pltpu.VMEM((tk, cols), F32), pltpu.VMEM((tk, cols), F32),
            pltpu.VMEM((tk, cols), BF16), pltpu.VMEM((tk, cols), BF16),
            pltpu.VMEM((1, cols), F32), pltpu.VMEM((1, cols), F32), pltpu.VMEM((1, cols), F32),
            pltpu.VMEM((v_dim, cols), F32)]


def _kv_tile(s):
    tk = min(KV_TILE, s // 2)
    assert s % (2 * tk) == 0
    return tk


def _diff_flash_kernel(q_ref, k_ref, vt_ref, lam_ref, sg_ref, o_ref, qs_ref, *scratch,
                       seq, tq, tk, half, lam_init):
    q = q_ref[...]
    lane = lax.broadcasted_iota(jnp.int32, q.shape, 1)
    zero = jnp.zeros_like(q)
    qs_ref[0:tq, :] = jnp.where(lane < half, q, zero)
    qs_ref[tq:2 * tq, :] = jnp.where(lane >= half, q, zero)
    ont = _flash_sweep(qs_ref, k_ref, vt_ref, scratch, seq=seq, tk=tk)
    lf = lam_ref[...]
    lam = (jnp.exp(jnp.sum(lf[0:1, :] * lf[1:2, :], axis=1, keepdims=True))
           - jnp.exp(jnp.sum(lf[2:3, :] * lf[3:4, :], axis=1, keepdims=True)) + lam_init)
    o = (ont[:, 0:tq] - lam * ont[:, tq:2 * tq]).T
    ms = jnp.mean(o * o, axis=-1, keepdims=True)
    o_ref[...] = ((o * lax.rsqrt(ms + EPS) * sg_ref[...]) * (1.0 - lam_init)).astype(BF16)


def _diff_flash_call(q, k, vt, lam, subln_g, layer_idx):
    b, s, d = q.shape
    v_dim = subln_g.shape[0]
    assert v_dim == LANES and d == DIFF_HEADS * v_dim
    tq, tk = min(Q_TILE, s), _kv_tile(s)
    lam_init = 0.8 - 0.6 * math.exp(-0.3 * layer_idx)
    kern = functools.partial(_diff_flash_kernel, seq=s, tq=tq, tk=tk, half=v_dim // 2, lam_init=lam_init)
    qo = pl.BlockSpec((None, tq, v_dim), lambda bi, h, i: (bi, i, h))
    return pl.pallas_call(
        kern,
        grid=(b, DIFF_HEADS, s // tq),
        in_specs=[qo,
                  pl.BlockSpec((None, s, v_dim), lambda bi, h, i: (bi, 0, h)),
                  pl.BlockSpec((None, v_dim, s), lambda bi, h, i: (bi, h, 0)),
                  pl.BlockSpec(lam.shape, lambda bi, h, i: (0, 0)),
                  pl.BlockSpec((1, v_dim), lambda bi, h, i: (0, 0))],
        out_specs=qo,
        out_shape=jax.ShapeDtypeStruct((b, s, d), BF16),
        scratch_shapes=_flash_scratch(2 * tq, tk, v_dim),
        compiler_params=_params("parallel", "parallel", "parallel"),
        name="diff_flash",
    )(q, k, vt, lam, subln_g.reshape(1, v_dim))


def _gqa_flash_kernel(q_ref, k_ref, vt_ref, o_ref, qs_ref, *scratch, seq, tq, tk, group):
    for r in range(group):
        qs_ref[r * tq:(r + 1) * tq, :] = q_ref[:, r * LANES:(r + 1) * LANES]
    ont = _flash_sweep(qs_ref, k_ref, vt_ref, scratch, seq=seq, tk=tk)
    for r in range(group):
        o_ref[:, r * LANES:(r + 1) * LANES] = ont[:, r * tq:(r + 1) * tq].T.astype(BF16)


def _gqa_flash_call(q, k, vt):
    b, s, d = q.shape
    hd = LANES
    group = GQA_HEADS // GQA_KV_HEADS
    tq, tk = min(GQA_Q_TILE, s), _kv_tile(s)
    kern = functools.partial(_gqa_flash_kernel, seq=s, tq=tq, tk=tk, group=group)
    qo = pl.BlockSpec((None, tq, group * hd), lambda bi, h, i: (bi, i, h))
    return pl.pallas_call(
        kern,
        grid=(b, GQA_KV_HEADS, s // tq),
        in_specs=[qo,
                  pl.BlockSpec((None, s, hd), lambda bi, h, i: (bi, 0, h)),
                  pl.BlockSpec((None, hd, s), lambda bi, h, i: (bi, h, 0))],
        out_specs=qo,
        out_shape=jax.ShapeDtypeStruct((b, s, d), BF16),
        scratch_shapes=_flash_scratch(group * tq, tk, hd),
        compiler_params=_params("parallel", "parallel", "parallel"),
        name="gqa_flash",
    )(q, k, vt)


def kernel(x, c, mod_w, mod_b, norm_g, ffn_w_gu, ffn_w_down, pool_w, pool_scale,
           diff_w_qkv, diff_lambda, diff_subln_g, diff_w_o,
           gqa_w_qkv, gqa_q_norm_g, gqa_k_norm_g, gqa_w_o,
           conv_w_in, conv_w, conv_w_out, final_g):
    b, s, d = x.shape
    depth = mod_w.shape[0]
    diff_head_dim = diff_lambda.shape[1]
    diff_tables = _diff_rope_tables(s, diff_head_dim, diff_head_dim // 4)
    axial_tables = _axial_rope_tables(s, gqa_q_norm_g.shape[0] // 2)

    mods = _mod_call(c, mod_w, mod_b).reshape(depth, b, N_MOD, d)
    w_gu, w_down = ffn_w_gu.astype(BF16), ffn_w_down.astype(BF16)

    for i in range(depth):
        m_ffn0, m_mix, m_ffn1 = (mods[i, :, 3 * k:3 * k + 3, :] for k in range(3))
        x = _ffn_call(x, m_ffn0, norm_g[i, 0], w_gu, w_down, i, 0, final_g)
        kind = i % 4
        attn = None
        if kind == 0:
            x = _pool_call(x, m_mix, norm_g[i, 1], pool_w.astype(BF16), pool_scale)
        elif kind == 1:
            q, k, vt = _diff_qkv_call(x, m_mix, norm_g[i, 1], diff_w_qkv.astype(BF16), diff_tables,
                                      diff_head_dim, diff_head_dim // 4)
            attn = (_diff_flash_call(q, k, vt, diff_lambda, diff_subln_g, i), m_mix, diff_w_o.astype(BF16))
        elif kind == 2:
            q, k, vt = _gqa_qkv_call(x, m_mix, norm_g[i, 1], gqa_w_qkv.astype(BF16),
                                     gqa_q_norm_g, gqa_k_norm_g, axial_tables)
            attn = (_gqa_flash_call(q, k, vt), m_mix, gqa_w_o.astype(BF16))
        else:
            x = _conv_call(x, m_mix, norm_g[i, 1], conv_w_in.astype(BF16), conv_w, conv_w_out.astype(BF16))
        x = _ffn_call(x, m_ffn1, norm_g[i, 2], w_gu, w_down, i, 1, final_g, final=(i == depth - 1), attn=attn)

    return x
```
